```python
import math
import jax, jax.numpy as jnp
from jax import lax
import numpy as np

D_MODEL = 4096
BATCH = 2
SEQ = 8192
DEPTH = 4

N_META = 16
D_MIX = D_MODEL
D_ATT = D_MIX // 4
D_CONV = D_MIX // 4
D_POOL = D_MIX // 4
D_SSM = D_MIX - D_ATT - D_CONV - D_POOL

ATT_HEADS = 8
ATT_VDIM = D_ATT // ATT_HEADS
ATT_QK = ATT_VDIM // 2
Q_BLOCK = 128

CONV_K = 31

POOL_WINDOWS = (2, 4, 8, 16)
POOL_GROUPS = len(POOL_WINDOWS)
POOL_GW = D_POOL // POOL_GROUPS

SSM_HEADDIM = 64
SSM_HEADS = D_SSM // SSM_HEADDIM
SSM_GROUPS = 4
SSM_HPG = SSM_HEADS // SSM_GROUPS
SSM_STATE = 128
SSM_CONV_K = 4
SSM_CHUNK = 128
SSM_XBC = D_SSM + 2 * SSM_GROUPS * SSM_STATE

D_IN = 3 * D_ATT + 2 * D_CONV + D_POOL + D_SSM + SSM_XBC + SSM_HEADS
SPLITS = [int(s) for s in np.cumsum([D_ATT, D_ATT, D_ATT, 2 * D_CONV, D_POOL, D_SSM, SSM_XBC])]

D_FF = 2 * D_MODEL
N_EXPERTS = 8
TOP_K = 2
D_FF_EXPERT = D_MODEL // 2
N_DENSE = (DEPTH + 1) // 2
N_MOE = DEPTH // 2

ALPHA = (2 * DEPTH) ** 0.25
BETA = (8 * DEPTH) ** -0.25
LN_EPS = 1e-5

kernel_name = "hybrid_diffattn_conformer_pool_ssd_moe_trunk"

F32 = jnp.float32


def layer_norm(x, g, b):
    xf = x.astype(F32)
    mu = jnp.mean(xf, axis=-1, keepdims=True)
    var = jnp.mean(jnp.square(xf - mu), axis=-1, keepdims=True)
    return ((xf - mu) * lax.rsqrt(var + LN_EPS) * g.astype(F32) + b.astype(F32)).astype(x.dtype)


def rms_norm(x, g):
    xf = x.astype(F32)
    return xf * lax.rsqrt(jnp.mean(xf * xf, axis=-1, keepdims=True) + LN_EPS) * g.astype(F32)


def causal_depthwise_conv(u, w, b):
    k_len = w.shape[0]
    out = lax.conv_general_dilated(
        u, w[:, None, :].astype(u.dtype), window_strides=(1,), padding=[(k_len - 1, 0)],
        dimension_numbers=('NWC', 'WIO', 'NWC'), feature_group_count=u.shape[-1])
    return out + b.astype(u.dtype)


def diff_attention(q, k, v, lam, lam_init, subln_g):
    bsz, seq_len = q.shape[0], q.shape[1]
    n_blk = -(-seq_len // Q_BLOCK)
    lp = n_blk * Q_BLOCK
    pad5 = ((0, 0), (0, lp - seq_len), (0, 0), (0, 0), (0, 0))
    qf = jnp.pad(q.astype(F32), pad5) * (ATT_QK ** -0.5)
    kf = jnp.pad(k.astype(F32), pad5)
    vf = jnp.pad(v.astype(F32), pad5[:4])
    q_blocks = qf.reshape(bsz, n_blk, Q_BLOCK, ATT_HEADS, 2, ATT_QK).transpose(1, 0, 3, 4, 2, 5)
    k_t = kf.transpose(0, 2, 3, 1, 4)
    v_t = vf.transpose(0, 2, 1, 3)
    slopes = 2.0 ** (-8.0 * jnp.arange(1, ATT_HEADS + 1, dtype=F32) / ATT_HEADS)
    k_pos = jnp.arange(lp)

    def block(args):
        q_blk, i = args
        q_pos = i * Q_BLOCK + jnp.arange(Q_BLOCK)
        dist = (q_pos[:, None] - k_pos[None, :]).astype(F32)
        s = jnp.einsum('bhiqd,bhikd->bhiqk', q_blk, k_t)
        s = s - slopes[None, :, None, None, None] * dist
        s = jnp.where(dist >= 0, s, -jnp.inf)
        p = jax.nn.softmax(s, axis=-1)
        a = p[:, :, 0] - lam * p[:, :, 1]
        return jnp.einsum('bhqk,bhkv->bqhv', a, v_t)

    out = lax.map(block, (q_blocks, jnp.arange(n_blk)))
    out = out.transpose(1, 0, 2, 3, 4).reshape(bsz, lp, ATT_HEADS, ATT_VDIM)[:, :seq_len]
    out = rms_norm(out, subln_g) * (1.0 - lam_init)
    return out.reshape(bsz, seq_len, D_ATT).astype(q.dtype)


def conformer_conv(u_glu, dw_w, dw_b, ln_g, ln_b, pw_w, pw_b):
    a, gate = jnp.split(u_glu, 2, axis=-1)
    h = a * jax.nn.sigmoid(gate)
    h = causal_depthwise_conv(h, dw_w, dw_b)
    h = jax.nn.silu(layer_norm(h, ln_g, ln_b))
    return h @ pw_w.astype(h.dtype) + pw_b.astype(h.dtype)


def multiscale_pool(u, w, bias, scale):
    bsz, seq_len = u.shape[0], u.shape[1]
    uf = u.astype(F32)
    cs = jnp.cumsum(uf, axis=1)
    pos = jnp.arange(1, seq_len + 1, dtype=F32)[None, :, None]
    means = []
    for gi, win in enumerate(POOL_WINDOWS):
        cs_g = cs[..., gi * POOL_GW:(gi + 1) * POOL_GW]
        lagged = jnp.pad(cs_g, ((0, 0), (win, 0), (0, 0)))[:, :seq_len]
        means.append((cs_g - lagged) / jnp.minimum(pos, float(win)))
    pooled = jnp.stack(means, axis=2)
    d = pooled - uf.reshape(bsz, seq_len, POOL_GROUPS, POOL_GW)
    y = jnp.einsum('blgc,gcd->blgd', d, w.astype(F32)).reshape(bsz, seq_len, D_POOL)
    return ((y + bias.astype(F32)) * scale.astype(F32)).astype(u.dtype)


def mamba2_ssd(z, xbc, dt_raw, conv_w, conv_b, dt_bias, a_log, d_skip, norm_g):
    bsz, seq_len = z.shape[0], z.shape[1]
    xbc = jax.nn.silu(causal_depthwise_conv(xbc, conv_w, conv_b))
    xs, bm, cm = jnp.split(xbc, [D_SSM, D_SSM + SSM_GROUPS * SSM_STATE], axis=-1)
    dt = jax.nn.softplus(dt_raw.astype(F32) + dt_bias.astype(F32))
    a = -jnp.exp(a_log.astype(F32)).reshape(SSM_GROUPS, SSM_HPG)
    pad = SSM_CHUNK - N_META
    n_c = (seq_len + pad) // SSM_CHUNK

    def front(t):
        return jnp.pad(t, ((0, 0), (pad, 0)) + ((0, 0),) * (t.ndim - 2))

    x_c = front(xs.astype(F32)).reshape(bsz, n_c, SSM_CHUNK, SSM_GROUPS, SSM_HPG, SSM_HEADDIM)
    b_c = front(bm.astype(F32)).reshape(bsz, n_c, SSM_CHUNK, SSM_GROUPS, SSM_STATE)
    c_c = front(cm.astype(F32)).reshape(bsz, n_c, SSM_CHUNK, SSM_GROUPS, SSM_STATE)
    dt_c = front(dt).reshape(bsz, n_c, SSM_CHUNK, SSM_GROUPS, SSM_HPG)
    a_cs = jnp.cumsum(dt_c * a, axis=2)
    x_dt = x_c * dt_c[..., None]
    seg = a_cs[:, :, :, None] - a_cs[:, :, None, :]
    causal = jnp.tril(jnp.ones((SSM_CHUNK, SSM_CHUNK), dtype=bool))[None, None, :, :, None, None]
    decay_ts = jnp.exp(jnp.where(causal, seg, -jnp.inf))
    cb = jnp.einsum('bctgn,bcsgn->bctsg', c_c, b_c)
    y_diag = jnp.einsum('bctsg,bctsgr,bcsgrp->bctgrp', cb, decay_ts, x_dt)
    decay_end = jnp.exp(a_cs[:, :, -1:] - a_cs)
    states = jnp.einsum('bctgn,bctgr,bctgrp->bcgrpn', b_c, decay_end, x_dt)
    chunk_decay = jnp.exp(a_cs[:, :, -1])

    def step(h, inp):
        st, dec = inp
        return dec[..., None, None] * h + st, h

    h0 = jnp.zeros((bsz, SSM_GROUPS, SSM_HPG, SSM_HEADDIM, SSM_STATE), F32)
    _, h_in = lax.scan(step, h0, (jnp.swapaxes(states, 0, 1), jnp.swapaxes(chunk_decay, 0, 1)))
    h_in = jnp.swapaxes(h_in, 0, 1)
    y_off = jnp.einsum('bctgn,bcgrpn,bctgr->bctgrp', c_c, h_in, jnp.exp(a_cs))
    y = (y_diag + y_off).reshape(bsz, n_c * SSM_CHUNK, SSM_HEADS, SSM_HEADDIM)[:, pad:]
    y = y + d_skip.astype(F32)[:, None] * xs.astype(F32).reshape(bsz, seq_len, SSM_HEADS, SSM_HEADDIM)
    y = y.reshape(bsz, seq_len, D_SSM) * jax.nn.silu(z.astype(F32))
    yg = y.reshape(bsz, seq_len, SSM_GROUPS, D_SSM // SSM_GROUPS)
    yg = yg * lax.rsqrt(jnp.mean(yg * yg, axis=-1, keepdims=True) + LN_EPS)
    return (yg.reshape(bsz, seq_len, D_SSM) * norm_g.astype(F32)).astype(z.dtype)


def swiglu(h, wg, wu, wd):
    return (jax.nn.silu(h @ wg.astype(h.dtype)) * (h @ wu.astype(h.dtype))) @ wd.astype(h.dtype)


def moe_swiglu(h, router, wg, wu, wd):
    logits = jnp.einsum('bld,de->ble', h.astype(F32), router.astype(F32))
    top_v, top_i = lax.top_k(logits, TOP_K)
    top_w = jax.nn.softmax(top_v, axis=-1)
    gates = jnp.sum(jax.nn.one_hot(top_i, N_EXPERTS, dtype=F32) * top_w[..., None], axis=-2)
    out = jnp.zeros(h.shape, F32)
    for e in range(N_EXPERTS):
        out = out + gates[..., e:e + 1] * swiglu(h, wg[e], wu[e], wd[e]).astype(F32)
    return out.astype(h.dtype)


def setup_inputs(seed: int = 0) -> dict:
    key = jax.random.key(seed)
    ks = iter(jax.random.split(key, 48))

    def nrm(shape, scale):
        return jax.random.normal(next(ks), shape, F32) * scale

    def gain(shape):
        return 1.0 + nrm(shape, 0.02)

    def bias(shape):
        return nrm(shape, 0.02)

    x = nrm((BATCH, SEQ, D_MODEL), 1.0)
    meta = nrm((N_META, D_MODEL), 1.0)
    emb_ln_g = gain((D_MODEL,))
    emb_ln_b = bias((D_MODEL,))
    w_in = nrm((DEPTH, D_MODEL, D_IN), D_MODEL ** -0.5)
    lam_q1 = nrm((DEPTH, ATT_QK), 0.1)
    lam_k1 = nrm((DEPTH, ATT_QK), 0.1)
    lam_q2 = nrm((DEPTH, ATT_QK), 0.1)
    lam_k2 = nrm((DEPTH, ATT_QK), 0.1)
    att_subln_g = gain((DEPTH, ATT_VDIM))
    cf_dw_w = nrm((DEPTH, CONV_K, D_CONV), CONV_K ** -0.5)
    cf_dw_b = bias((DEPTH, D_CONV))
    cf_ln_g = gain((DEPTH, D_CONV))
    cf_ln_b = bias((DEPTH, D_CONV))
    cf_pw_w = nrm((DEPTH, D_CONV, D_CONV), D_CONV ** -0.5)
    cf_pw_b = bias((DEPTH, D_CONV))
    pool_w = nrm((DEPTH, POOL_GROUPS, POOL_GW, POOL_GW), POOL_GW ** -0.5)
    pool_b = bias((DEPTH, D_POOL))
    pool_scale = 1.0 + nrm((DEPTH, D_POOL), 0.1)
    ssm_conv_w = nrm((DEPTH, SSM_CONV_K, SSM_XBC), SSM_CONV_K ** -0.5)
    ssm_conv_b = bias((DEPTH, SSM_XBC))
    dt0 = jnp.exp(jax.random.uniform(next(ks), (DEPTH, SSM_HEADS), F32, math.log(1e-3), math.log(1e-1)))
    ssm_dt_bias = dt0 + jnp.log(-jnp.expm1(-dt0))
    ssm_a_log = jnp.log(jax.random.uniform(next(ks), (DEPTH, SSM_HEADS), F32, 1.0, 16.0))
    ssm_d = gain((DEPTH, SSM_HEADS))
    ssm_norm_g = gain((DEPTH, D_SSM))
    w_out = nrm((DEPTH, D_MIX, D_MODEL), BETA * D_MIX ** -0.5)
    ln1_g = gain((DEPTH, D_MODEL))
    ln1_b = bias((DEPTH, D_MODEL))
    ln2_g = gain((DEPTH, D_MODEL))
    ln2_b = bias((DEPTH, D_MODEL))
    ffn_w_gate = nrm((N_DENSE, D_MODEL, D_FF), D_MODEL ** -0.5)
    ffn_w_up = nrm((N_DENSE, D_MODEL, D_FF), D_MODEL ** -0.5)
    ffn_w_down = nrm((N_DENSE, D_FF, D_MODEL), BETA * D_FF ** -0.5)
    moe_router = nrm((N_MOE, D_MODEL, N_EXPERTS), D_MODEL ** -0.5)
    moe_w_gate = nrm((N_MOE, N_EXPERTS, D_MODEL, D_FF_EXPERT), D_MODEL ** -0.5)
    moe_w_up = nrm((N_MOE, N_EXPERTS, D_MODEL, D_FF_EXPERT), D_MODEL ** -0.5)
    moe_w_down = nrm((N_MOE, N_EXPERTS, D_FF_EXPERT, D_MODEL), BETA * D_FF_EXPERT ** -0.5)
    return {
        'x': x, 'meta': meta, 'emb_ln_g': emb_ln_g, 'emb_ln_b': emb_ln_b, 'w_in': w_in,
        'lam_q1': lam_q1, 'lam_k1': lam_k1, 'lam_q2': lam_q2, 'lam_k2': lam_k2,
        'att_subln_g': att_subln_g,
        'cf_dw_w': cf_dw_w, 'cf_dw_b': cf_dw_b, 'cf_ln_g': cf_ln_g, 'cf_ln_b': cf_ln_b,
        'cf_pw_w': cf_pw_w, 'cf_pw_b': cf_pw_b,
        'pool_w': pool_w, 'pool_b': pool_b, 'pool_scale': pool_scale,
        'ssm_conv_w': ssm_conv_w, 'ssm_conv_b': ssm_conv_b, 'ssm_dt_bias': ssm_dt_bias,
        'ssm_a_log': ssm_a_log, 'ssm_d': ssm_d, 'ssm_norm_g': ssm_norm_g,
        'w_out': w_out, 'ln1_g': ln1_g, 'ln1_b': ln1_b, 'ln2_g': ln2_g, 'ln2_b': ln2_b,
        'ffn_w_gate': ffn_w_gate, 'ffn_w_up': ffn_w_up, 'ffn_w_down': ffn_w_down,
        'moe_router': moe_router, 'moe_w_gate': moe_w_gate, 'moe_w_up': moe_w_up,
        'moe_w_down': moe_w_down,
    }


def reference(x, meta, emb_ln_g, emb_ln_b, w_in, lam_q1, lam_k1, lam_q2, lam_k2, att_subln_g,
              cf_dw_w, cf_dw_b, cf_ln_g, cf_ln_b, cf_pw_w, cf_pw_b,
              pool_w, pool_b, pool_scale,
              ssm_conv_w, ssm_conv_b, ssm_dt_bias, ssm_a_log, ssm_d, ssm_norm_g,
              w_out, ln1_g, ln1_b, ln2_g, ln2_b,
              ffn_w_gate, ffn_w_up, ffn_w_down,
              moe_router, moe_w_gate, moe_w_up, moe_w_down):
    bsz = x.shape[0]
    meta_b = jnp.broadcast_to(meta[None].astype(x.dtype), (bsz, N_META, D_MODEL))
    h = layer_norm(jnp.concatenate([meta_b, x], axis=1), emb_ln_g, emb_ln_b)
    seq_len = h.shape[1]
    for l in range(DEPTH):
        proj = h @ w_in[l].astype(h.dtype)
        q, k, v, u_glu, u_pool, z, xbc, dt_raw = jnp.split(proj, SPLITS, axis=-1)
        lam_init = 0.8 - 0.6 * math.exp(-0.3 * l)
        lam = (jnp.exp(jnp.sum(lam_q1[l].astype(F32) * lam_k1[l].astype(F32)))
               - jnp.exp(jnp.sum(lam_q2[l].astype(F32) * lam_k2[l].astype(F32))) + lam_init)
        y_att = diff_attention(q.reshape(bsz, seq_len, ATT_HEADS, 2, ATT_QK),
                               k.reshape(bsz, seq_len, ATT_HEADS, 2, ATT_QK),
                               v.reshape(bsz, seq_len, ATT_HEADS, ATT_VDIM),
                               lam, lam_init, att_subln_g[l])
        y_conv = conformer_conv(u_glu, cf_dw_w[l], cf_dw_b[l], cf_ln_g[l], cf_ln_b[l], cf_pw_w[l], cf_pw_b[l])
        y_pool = multiscale_pool(u_pool, pool_w[l], pool_b[l], pool_scale[l])
        y_ssm = mamba2_ssd(z, xbc, dt_raw, ssm_conv_w[l], ssm_conv_b[l], ssm_dt_bias[l],
                           ssm_a_log[l], ssm_d[l], ssm_norm_g[l])
        mix = jnp.concatenate([y_att, y_conv, y_pool, y_ssm], axis=-1) @ w_out[l].astype(h.dtype)
        h = layer_norm(ALPHA * h + mix, ln1_g[l], ln1_b[l])
        if l % 2 == 0:
            f = swiglu(h, ffn_w_gate[l // 2], ffn_w_up[l // 2], ffn_w_down[l // 2])
        else:
            f = moe_swiglu(h, moe_router[l // 2], moe_w_gate[l // 2], moe_w_up[l // 2], moe_w_down[l // 2])
        h = layer_norm(ALPHA * h + f, ln2_g[l], ln2_b[l])
    return h[:, N_META:]
```

```python
import functools
import math

import numpy as np
import jax
import jax.numpy as jnp
from jax import lax
from jax.experimental import pallas as pl
from jax.experimental.pallas import tpu as pltpu

F32 = jnp.float32
BF16 = jnp.bfloat16
HIGHEST = lax.Precision.HIGHEST

N_META = 16
ATT_HEADS = 8
CONV_K = 31
POOL_WINDOWS = (2, 4, 8, 16)
SSM_HEADDIM = 64
SSM_GROUPS = 4
SSM_STATE = 128
SSM_CONV_K = 4
N_EXPERTS = 8
LN_EPS = 1e-5

_LANE = 128
_SUBLANE = 8
_VMEM_CAP = 60 * 1024 * 1024

_SEQ_ALIGN = 256
_TM_TARGET = 1536
_TS_TARGET = 768
_TK_ATT = 256
_SSD_CHUNK = 128
_LN_ROWS = 256
_NEG = -1e30


def _divisor_tile(n, target, align):
    best = align
    t = align
    while t <= min(n, target):
        if n % t == 0:
            best = t
        t += align
    assert n % best == 0, (n, target, align)
    return best


def _params(sem, vmem_bytes):
    return pltpu.CompilerParams(
        dimension_semantics=sem,
        vmem_limit_bytes=int(min(max(vmem_bytes, 32 * 1024 * 1024), _VMEM_CAP)))


def _nbytes(shape, dtype):
    return int(np.prod(shape)) * jnp.dtype(dtype).itemsize


def _ln_kernel(z_ref, g_ref, b_ref, hf_ref, hb_ref):
    z = z_ref[...]
    mu = jnp.mean(z, axis=-1, keepdims=True)
    zc = z - mu
    var = jnp.mean(zc * zc, axis=-1, keepdims=True)
    y = zc * lax.rsqrt(var + LN_EPS) * g_ref[...] + b_ref[...]
    hf_ref[...] = y
    hb_ref[...] = y.astype(BF16)


def _layer_norm(z, g, b):
    m, d = z.shape
    tm = _divisor_tile(m, _LN_ROWS, _SUBLANE)
    row = pl.BlockSpec((tm, d), lambda i: (i, 0))
    vec = pl.BlockSpec((1, d), lambda i: (0, 0))
    return pl.pallas_call(
        _ln_kernel,
        grid=(m // tm,),
        in_specs=[row, vec, vec],
        out_specs=[row, row],
        out_shape=[jax.ShapeDtypeStruct((m, d), F32), jax.ShapeDtypeStruct((m, d), BF16)],
        compiler_params=_params(("parallel",), 8 * _nbytes((tm, d), F32)),
        name="layer_norm",
    )(z, g.reshape(1, d).astype(F32), b.reshape(1, d).astype(F32))


def _mm_kernel(*refs, n_in, alpha, has_res):
    x_refs = refs[:n_in]
    w_refs = refs[n_in:2 * n_in]
    o_ref = refs[-1]
    acc = None
    for x_ref, w_ref in zip(x_refs, w_refs):
        d = jnp.dot(x_ref[...], w_ref[...], preferred_element_type=F32)
        acc = d if acc is None else acc + d
    if has_res:
        acc = acc + alpha * refs[2 * n_in][...]
    o_ref[...] = acc.astype(o_ref.dtype)


def _matmul(xs, w, out_dtype, residual=None, alpha=1.0):
    n_in = len(xs)
    m, kp = xs[0].shape
    k_total, n = w.shape
    assert k_total == n_in * kp
    tm = _divisor_tile(m, _TM_TARGET if k_total <= 4096 else _TM_TARGET // 2, _LANE)
    tn = _divisor_tile(n, 512 if k_total <= 4096 else 256, _LANE)
    if k_total > 8192:
        tm = _divisor_tile(m, _TM_TARGET // 4, _LANE)
    in_specs = [pl.BlockSpec((tm, kp), lambda i, j: (i, 0)) for _ in range(n_in)]
    in_specs += [pl.BlockSpec((kp, tn), lambda i, j, p=p: (p, j)) for p in range(n_in)]
    args = list(xs) + [w] * n_in
    vmem = 2 * (n_in * _nbytes((tm, kp), BF16) + n_in * _nbytes((kp, tn), BF16)
                + _nbytes((tm, tn), out_dtype)) + 2 * _nbytes((tm, tn), F32)
    if residual is not None:
        in_specs.append(pl.BlockSpec((tm, tn), lambda i, j: (i, j)))
        args.append(residual)
        vmem += 2 * _nbytes((tm, tn), F32)
    return pl.pallas_call(
        functools.partial(_mm_kernel, n_in=n_in, alpha=alpha, has_res=residual is not None),
        grid=(m // tm, n // tn),
        in_specs=in_specs,
        out_specs=pl.BlockSpec((tm, tn), lambda i, j: (i, j)),
        out_shape=jax.ShapeDtypeStruct((m, n), out_dtype),
        compiler_params=_params(("parallel", "arbitrary"), vmem + (4 << 20)),
        name="matmul",
    )(*args)


def _glu_kernel(*refs, gated):
    if gated:
        x_ref, wg_ref, wu_ref, gates_ref, o_ref = refs
    else:
        x_ref, wg_ref, wu_ref, o_ref = refs
    x = x_ref[...]
    g = jnp.dot(x, wg_ref[...], preferred_element_type=F32)
    u = jnp.dot(x, wu_ref[...], preferred_element_type=F32)
    a = g * jax.nn.sigmoid(g) * u
    if gated:
        gates = gates_ref[...]
        lane = lax.broadcasted_iota(jnp.int32, gates.shape, 1)
        col = jnp.sum(jnp.where(lane == pl.program_id(1), gates, 0.0), axis=1, keepdims=True)
        a = a * col
    o_ref[...] = a.astype(o_ref.dtype)


def _swiglu_up(x, wg, wu, gates=None):
    m, d = x.shape
    n_e, _, f = wg.shape
    tm = _divisor_tile(m, _TM_TARGET, _LANE)
    tn = _divisor_tile(f, 256, _LANE)
    nj = f // tn
    in_specs = [
        pl.BlockSpec((tm, d), lambda i, e, j: (i, 0)),
        pl.BlockSpec((None, d, tn), lambda i, e, j: (e, 0, j)),
        pl.BlockSpec((None, d, tn), lambda i, e, j: (e, 0, j)),
    ]
    args = [x, wg, wu]
    if gates is not None:
        in_specs.append(pl.BlockSpec((tm, _LANE), lambda i, e, j: (i, 0)))
        args.append(gates)
    vmem = 2 * (_nbytes((tm, d), BF16) + 2 * _nbytes((d, tn), BF16) + _nbytes((tm, tn), BF16)) \
        + 4 * _nbytes((tm, tn), F32)
    return pl.pallas_call(
        functools.partial(_glu_kernel, gated=gates is not None),
        grid=(m // tm, n_e, nj),
        in_specs=in_specs,
        out_specs=pl.BlockSpec((tm, tn), lambda i, e, j: (i, e * nj + j)),
        out_shape=jax.ShapeDtypeStruct((m, n_e * f), BF16),
        compiler_params=_params(("parallel", "arbitrary", "arbitrary"), vmem + (4 << 20)),
        name="swiglu_up",
    )(*args)


def _router_kernel(h_ref, r_ref, o_ref, *, n_exp):
    logits = jnp.dot(h_ref[...], r_ref[...], preferred_element_type=F32, precision=HIGHEST)
    lane = lax.broadcasted_iota(jnp.int32, logits.shape, 1).astype(F32)
    lg = jnp.where(lane < n_exp, logits, _NEG)
    m1 = jnp.max(lg, axis=1, keepdims=True)
    i1 = jnp.min(jnp.where(lg == m1, lane, float(_LANE)), axis=1, keepdims=True)
    lg2 = jnp.where(lane == i1, _NEG, lg)
    m2 = jnp.max(lg2, axis=1, keepdims=True)
    i2 = jnp.min(jnp.where(lg2 == m2, lane, float(_LANE)), axis=1, keepdims=True)
    e = jnp.exp(m2 - m1)
    w1 = 1.0 / (1.0 + e)
    w2 = e / (1.0 + e)
    o_ref[...] = jnp.where(lane == i1, w1, 0.0) + jnp.where(lane == i2, w2, 0.0)


def _router_gates(h, router):
    m, d = h.shape
    n_exp = router.shape[1]
    tm = _divisor_tile(m, 512, _SUBLANE)
    r_pad = jnp.pad(router.astype(F32), ((0, 0), (0, _LANE - n_exp)))
    return pl.pallas_call(
        functools.partial(_router_kernel, n_exp=n_exp),
        grid=(m // tm,),
        in_specs=[pl.BlockSpec((tm, d), lambda i: (i, 0)), pl.BlockSpec((d, _LANE), lambda i: (0, 0))],
        out_specs=pl.BlockSpec((tm, _LANE), lambda i: (i, 0)),
        out_shape=jax.ShapeDtypeStruct((m, _LANE), F32),
        compiler_params=_params(("parallel",), 4 * _nbytes((tm, d), F32) + (8 << 20)),
        name="router",
    )(h, r_pad)


def _attn_kernel(slopes_ref, q_ref, k_ref, v_ref, lq1_ref, lk1_ref, lq2_ref, lk2_ref, g_ref,
                 o_ref, m_ref, l_ref, acc_ref, *, tq, tk, qk_dim, scale, lam_init):
    h = pl.program_id(1)
    i = pl.program_id(2)
    slope = slopes_ref[h]
    r = tq // tk

    qs = (q_ref[...].astype(F32) * scale).astype(BF16)
    lane = lax.broadcasted_iota(jnp.int32, qs.shape, 1)
    zero = jnp.zeros_like(qs)
    qq = jnp.concatenate([jnp.where(lane < qk_dim, qs, zero),
                          jnp.where(lane >= qk_dim, qs, zero)], axis=0)

    row = lax.broadcasted_iota(jnp.int32, (2 * tq, tk), 0)
    col = lax.broadcasted_iota(jnp.int32, (2 * tq, tk), 1)
    rel = (col - jnp.where(row >= tq, row - tq, row)).astype(F32)
    sd = slope * rel

    m_ref[...] = jnp.full(m_ref.shape, _NEG, F32)
    l_ref[...] = jnp.zeros(l_ref.shape, F32)
    acc_ref[...] = jnp.zeros(acc_ref.shape, F32)

    def step(j, masked):
        k0 = pl.multiple_of(j * tk, tk)
        kj = k_ref[pl.ds(k0, tk), :]
        vj = v_ref[pl.ds(k0, tk), :]
        s = lax.dot_general(qq, kj, (((1,), (1,)), ((), ())), preferred_element_type=F32) + sd
        off = j * tk - i * tq
        if masked:
            s = jnp.where(rel <= (-off).astype(F32), s, _NEG)
        c = slope * off.astype(F32)
        m_old = m_ref[...]
        m_new = jnp.maximum(m_old, jnp.max(s, axis=1, keepdims=True) + c)
        p = jnp.exp(s + (c - m_new))
        alpha = jnp.exp(m_old - m_new)
        l_ref[...] = alpha * l_ref[...] + jnp.sum(p, axis=1, keepdims=True)
        acc_ref[...] = alpha * acc_ref[...] + jnp.dot(p.astype(BF16), vj, preferred_element_type=F32)
        m_ref[...] = m_new

    def body(j, carry):
        step(j, False)
        return carry

    lax.fori_loop(0, i * r, body, 0)
    for d in range(r):
        step(i * r + d, True)

    o12 = acc_ref[...] / l_ref[...]
    lam = (jnp.exp(jnp.sum(lq1_ref[...] * lk1_ref[...], axis=1, keepdims=True))
           - jnp.exp(jnp.sum(lq2_ref[...] * lk2_ref[...], axis=1, keepdims=True)) + lam_init)
    o = o12[:tq] - lam * o12[tq:]
    ms = jnp.mean(o * o, axis=1, keepdims=True)
    o_ref[...] = (o * lax.rsqrt(ms + LN_EPS) * g_ref[...] * (1.0 - lam_init)).astype(o_ref.dtype)


def _diff_attention(proj, bsz, lp, d_att, lam_vecs, subln_g, lam_init):
    vdim = d_att // ATT_HEADS
    qk_dim = vdim // 2
    assert vdim == _LANE
    tq = _divisor_tile(lp, _TS_TARGET, _SEQ_ALIGN)
    tk = _divisor_tile(tq, _TK_ATT, _LANE)
    nq = lp // tq
    slopes = jnp.asarray(2.0 ** (-8.0 * np.arange(1, ATT_HEADS + 1) / ATT_HEADS), F32)
    lq1, lk1, lq2, lk2 = [v.reshape(1, qk_dim).astype(F32) for v in lam_vecs]
    vec = pl.BlockSpec((1, qk_dim), lambda b, h, i: (0, 0))
    in_specs = [
        pl.BlockSpec(memory_space=pltpu.SMEM),
        pl.BlockSpec((tq, vdim), lambda b, h, i: (b * nq + i, h)),
        pl.BlockSpec((lp, vdim), lambda b, h, i: (b, ATT_HEADS + h)),
        pl.BlockSpec((lp, vdim), lambda b, h, i: (b, 2 * ATT_HEADS + h)),
        vec, vec, vec, vec,
        pl.BlockSpec((1, vdim), lambda b, h, i: (0, 0)),
    ]
    vmem = 2 * (2 * _nbytes((lp, vdim), BF16) + 2 * _nbytes((tq, vdim), BF16)) \
        + 3 * _nbytes((2 * tq, _LANE), F32) + 6 * _nbytes((2 * tq, tk), F32)
    return pl.pallas_call(
        functools.partial(_attn_kernel, tq=tq, tk=tk, qk_dim=qk_dim, scale=qk_dim ** -0.5,
                          lam_init=lam_init),
        grid=(bsz, ATT_HEADS, nq),
        in_specs=in_specs,
        out_specs=pl.BlockSpec((tq, vdim), lambda b, h, i: (b * nq + i, h)),
        out_shape=jax.ShapeDtypeStruct((bsz * lp, d_att), BF16),
        scratch_shapes=[pltpu.VMEM((2 * tq, 1), F32), pltpu.VMEM((2 * tq, 1), F32),
                        pltpu.VMEM((2 * tq, vdim), F32)],
        compiler_params=_params(("parallel", "parallel", "arbitrary"), vmem + (8 << 20)),
        name="diff_attention",
    )(slopes, proj, proj, proj, lq1, lk1, lq2, lk2, subln_g.reshape(1, vdim).astype(F32))


_CONV_HALO = 32
_CONV_ROWS = 32


def _conv_kernel(a_ref, gate_ref, dw_ref, dwb_ref, lng_ref, lnb_ref, pw_ref, pwb_ref, o_ref,
                 hbuf_ref, cbuf_ref, sh_ref, *, ts, k_len):
    t = pl.program_id(1)
    c_dim = a_ref.shape[1]

    @pl.when(t == 0)
    def _():
        hbuf_ref[0:_CONV_HALO, :] = jnp.zeros((_CONV_HALO, c_dim), F32)

    @pl.when(t > 0)
    def _():
        hbuf_ref[0:_CONV_HALO, :] = hbuf_ref[ts:ts + _CONV_HALO, :]

    a = a_ref[...].astype(F32)
    hbuf_ref[_CONV_HALO:_CONV_HALO + ts, :] = a * jax.nn.sigmoid(gate_ref[...].astype(F32))

    first = _CONV_HALO - (k_len - 1)
    sh_rows = sh_ref.shape[1]
    for lb in range(c_dim // _LANE):
        lanes = slice(lb * _LANE, (lb + 1) * _LANE)
        for b in range(1, _SUBLANE):
            sh_ref[b - 1] = hbuf_ref[b:b + sh_rows, lanes]
        taps = [dw_ref[k:k + 1, lanes] for k in range(k_len)]
        bias = dwb_ref[:, lanes]

        def chunk(c, carry, lanes=lanes, taps=taps, bias=bias):
            base = c * _CONV_ROWS
            acc = jnp.broadcast_to(bias, (_CONV_ROWS, _LANE))
            for k in range(k_len):
                a, b = divmod(first + k, _SUBLANE)
                start = pl.multiple_of(base + a * _SUBLANE, _SUBLANE)
                if b == 0:
                    src = hbuf_ref[pl.ds(start, _CONV_ROWS), lanes]
                else:
                    src = sh_ref[b - 1, pl.ds(start, _CONV_ROWS), :]
                acc = acc + taps[k] * src
            cbuf_ref[pl.ds(pl.multiple_of(base, _CONV_ROWS), _CONV_ROWS), lanes] = acc
            return carry

        lax.fori_loop(0, ts // _CONV_ROWS, chunk, 0)

    y = cbuf_ref[...]
    mu = jnp.mean(y, axis=-1, keepdims=True)
    yc = y - mu
    var = jnp.mean(yc * yc, axis=-1, keepdims=True)
    yn = yc * lax.rsqrt(var + LN_EPS) * lng_ref[...] + lnb_ref[...]
    act = (yn * jax.nn.sigmoid(yn)).astype(BF16)
    out = jnp.dot(act, pw_ref[...], preferred_element_type=F32) + pwb_ref[...]
    o_ref[...] = out.astype(o_ref.dtype)


def _conformer_conv(proj, bsz, lp, col0, d_conv, dw_w, dw_b, ln_g, ln_b, pw_w, pw_b):
    ts = _divisor_tile(lp, _TS_TARGET, _CONV_ROWS)
    nt = lp // ts
    cb = col0 // d_conv
    assert col0 % d_conv == 0 and dw_w.shape[0] - 1 <= _CONV_HALO
    vec = pl.BlockSpec((1, d_conv), lambda b, t: (0, 0))
    in_specs = [
        pl.BlockSpec((ts, d_conv), lambda b, t: (b * nt + t, cb)),
        pl.BlockSpec((ts, d_conv), lambda b, t: (b * nt + t, cb + 1)),
        pl.BlockSpec((dw_w.shape[0], d_conv), lambda b, t: (0, 0)),
        vec, vec, vec,
        pl.BlockSpec((d_conv, d_conv), lambda b, t: (0, 0)),
        vec,
    ]
    vmem = 4 * _nbytes((ts, d_conv), BF16) * 2 + 8 * _nbytes((ts + _CONV_HALO, d_conv), F32)
    r1 = lambda v: v.reshape(1, d_conv).astype(F32)
    return pl.pallas_call(
        functools.partial(_conv_kernel, ts=ts, k_len=dw_w.shape[0]),
        grid=(bsz, nt),
        in_specs=in_specs,
        out_specs=pl.BlockSpec((ts, d_conv), lambda b, t: (b * nt + t, 0)),
        out_shape=jax.ShapeDtypeStruct((bsz * lp, d_conv), BF16),
        scratch_shapes=[pltpu.VMEM((ts + _CONV_HALO, d_conv), F32), pltpu.VMEM((ts, d_conv), F32),
                        pltpu.VMEM((_SUBLANE - 1, ts + _CONV_HALO - _SUBLANE, _LANE), F32)],
        compiler_params=_params(("parallel", "arbitrary"), vmem),
        name="conformer_conv",
    )(proj, proj, dw_w.astype(F32), r1(dw_b), r1(ln_g), r1(ln_b), pw_w.astype(BF16), r1(pw_b))


_POOL_HALO = 16


def _pool_kernel(u_ref, w_ref, b_ref, s_ref, o_ref, buf_ref, *, ts, windows, gw):
    t = pl.program_id(1)
    c_dim = u_ref.shape[1]

    @pl.when(t == 0)
    def _():
        buf_ref[0:_POOL_HALO, :] = jnp.zeros((_POOL_HALO, c_dim), F32)

    @pl.when(t > 0)
    def _():
        buf_ref[0:_POOL_HALO, :] = buf_ref[ts:ts + _POOL_HALO, :]

    buf_ref[_POOL_HALO:_POOL_HALO + ts, :] = u_ref[...].astype(F32)
    pos = (t * ts + 1 + lax.broadcasted_iota(jnp.int32, (ts, 1), 0)).astype(F32)
    for gi, win in enumerate(windows):
        lanes = slice(gi * gw, (gi + 1) * gw)
        u = buf_ref[_POOL_HALO:_POOL_HALO + ts, lanes]
        tot = u
        for j in range(1, win):
            tot = tot + buf_ref[_POOL_HALO - j:_POOL_HALO - j + ts, lanes]
        d = tot / jnp.minimum(pos, float(win)) - u
        y = jnp.dot(d.astype(BF16), w_ref[gi], preferred_element_type=F32)
        o_ref[:, lanes] = ((y + b_ref[:, lanes]) * s_ref[:, lanes]).astype(o_ref.dtype)


def _multiscale_pool(proj, bsz, lp, col0, d_pool, w, bias, scale):
    n_g = len(POOL_WINDOWS)
    gw = d_pool // n_g
    ts = _divisor_tile(lp, _TS_TARGET, _SUBLANE)
    nt = lp // ts
    assert col0 % d_pool == 0 and max(POOL_WINDOWS) - 1 <= _POOL_HALO
    cb = col0 // d_pool
    vec = pl.BlockSpec((1, d_pool), lambda b, t: (0, 0))
    r1 = lambda v: v.reshape(1, d_pool).astype(F32)
    return pl.pallas_call(
        functools.partial(_pool_kernel, ts=ts, windows=POOL_WINDOWS, gw=gw),
        grid=(bsz, nt),
        in_specs=[pl.BlockSpec((ts, d_pool), lambda b, t: (b * nt + t, cb)),
                  pl.BlockSpec((n_g, gw, gw), lambda b, t: (0, 0, 0)), vec, vec],
        out_specs=pl.BlockSpec((ts, d_pool), lambda b, t: (b * nt + t, 0)),
        out_shape=jax.ShapeDtypeStruct((bsz * lp, d_pool), BF16),
        scratch_shapes=[pltpu.VMEM((ts + _POOL_HALO, d_pool), F32)],
        compiler_params=_params(("parallel", "arbitrary"), 10 * _nbytes((ts, d_pool), F32)),
        name="multiscale_pool",
    )(proj, w.astype(BF16), r1(bias), r1(scale))


_SSD_HALO = 8


def _ssd_kernel(z_ref, x_ref, bc_ref, dt_ref, cw_ref, cb_ref, dtb_ref, alog_ref, dskip_ref, ng_ref,
                expand_ref, o_ref, ext_ref, state_ref, *, tc, d_ssm, k_len):
    c = pl.program_id(1)
    n_state = SSM_STATE
    gwid = d_ssm // SSM_GROUPS
    hpg = gwid // SSM_HEADDIM
    xbc_dim = x_ref.shape[1] + bc_ref.shape[1]

    @pl.when(c == 0)
    def _():
        ext_ref[0:_SSD_HALO, :] = jnp.zeros((_SSD_HALO, xbc_dim), F32)
        state_ref[...] = jnp.zeros(state_ref.shape, F32)

    @pl.when(c > 0)
    def _():
        ext_ref[0:_SSD_HALO, :] = ext_ref[tc:tc + _SSD_HALO, :]

    ext_ref[_SSD_HALO:_SSD_HALO + tc, :d_ssm] = x_ref[...].astype(F32)
    ext_ref[_SSD_HALO:_SSD_HALO + tc, d_ssm:] = bc_ref[...].astype(F32)

    first = _SSD_HALO - (k_len - 1)
    pre = jnp.broadcast_to(cb_ref[...], (tc, xbc_dim))
    for k in range(k_len):
        pre = pre + cw_ref[k:k + 1, :] * ext_ref[first + k:first + k + tc, :]
    xbc = pre * jax.nn.sigmoid(pre)
    xs = xbc[:, :d_ssm]
    bm = xbc[:, d_ssm:d_ssm + SSM_GROUPS * n_state].astype(BF16)
    cm = xbc[:, d_ssm + SSM_GROUPS * n_state:].astype(BF16)

    dt = jax.nn.softplus(dt_ref[...] + dtb_ref[...])
    a = dt * (-jnp.exp(alog_ref[...]))
    ti = lax.broadcasted_iota(jnp.int32, (tc, tc), 0)
    si = lax.broadcasted_iota(jnp.int32, (tc, tc), 1)
    causal = ti >= si
    tri = jnp.where(causal, 1.0, 0.0).astype(F32)
    a_cs = jnp.dot(tri, a, preferred_element_type=F32, precision=HIGHEST)
    a_cs_t = a_cs.T
    expand = expand_ref[...]
    a_cs_x = jnp.dot(a_cs, expand, preferred_element_type=F32, precision=HIGHEST)
    dt_x = jnp.dot(dt, expand, preferred_element_type=F32, precision=HIGHEST)
    a_last = a_cs_x[tc - 1:tc, :]
    x_dt = xs * dt_x
    x_end = (x_dt * jnp.exp(a_last - a_cs_x)).astype(BF16)
    x_dt_b = x_dt.astype(BF16)
    decay_in = jnp.exp(a_cs_x)
    decay_chunk = jnp.exp(a_last)

    lane_g = lax.broadcasted_iota(jnp.int32, (tc, gwid), 1)
    ys = []
    for g in range(SSM_GROUPS):
        cg = cm[:, g * n_state:(g + 1) * n_state]
        bg = bm[:, g * n_state:(g + 1) * n_state]
        gl = slice(g * gwid, (g + 1) * gwid)
        cb = lax.dot_general(cg, bg, (((1,), (1,)), ((), ())), preferred_element_type=F32)
        xg = x_dt_b[:, gl]
        y_g = None
        for r in range(hpg):
            hd = g * hpg + r
            seg = a_cs[:, hd:hd + 1] - a_cs_t[hd:hd + 1, :]
            lmat = jnp.exp(jnp.where(causal, seg, _NEG))
            mh = (cb * lmat).astype(BF16)
            in_head = (lane_g >= r * SSM_HEADDIM) & (lane_g < (r + 1) * SSM_HEADDIM)
            xh = jnp.where(in_head, xg, jnp.zeros_like(xg))
            yh = jnp.dot(mh, xh, preferred_element_type=F32)
            y_g = yh if y_g is None else y_g + yh
        st = state_ref[g]
        y_g = y_g + jnp.dot(cg, st.astype(BF16), preferred_element_type=F32) * decay_in[:, gl]
        upd = lax.dot_general(bg, x_end[:, gl], (((0,), (0,)), ((), ())), preferred_element_type=F32)
        state_ref[g] = decay_chunk[:, gl] * st + upd
        ys.append(y_g)
    y = jnp.concatenate(ys, axis=1) + dskip_ref[...] * xs
    zf = z_ref[...].astype(F32)
    y = y * (zf * jax.nn.sigmoid(zf))
    outs = []
    for g in range(SSM_GROUPS):
        yg = y[:, g * gwid:(g + 1) * gwid]
        ms = jnp.mean(yg * yg, axis=-1, keepdims=True)
        outs.append(yg * lax.rsqrt(ms + LN_EPS))
    o_ref[...] = (jnp.concatenate(outs, axis=1) * ng_ref[...]).astype(o_ref.dtype)


def _mamba2_ssd(proj, dt_raw, bsz, lp, z_col0, d_ssm, conv_w, conv_b, dt_bias, a_log, d_skip, norm_g):
    xbc_dim = d_ssm + 2 * SSM_GROUPS * SSM_STATE
    n_heads = d_ssm // SSM_HEADDIM
    tc = _SSD_CHUNK
    nc = lp // tc
    bc_dim = xbc_dim - d_ssm
    assert lp % tc == 0 and z_col0 % d_ssm == 0 and (z_col0 + 2 * d_ssm) % bc_dim == 0
    zb = z_col0 // d_ssm
    bcb = (z_col0 + 2 * d_ssm) // bc_dim
    pad_h = lambda v: jnp.pad(v.astype(F32).reshape(1, n_heads), ((0, 0), (0, _LANE - n_heads)))
    expand = (np.arange(d_ssm)[None, :] // SSM_HEADDIM == np.arange(_LANE)[:, None]).astype(np.float32)
    d_x = jnp.repeat(d_skip.astype(F32), SSM_HEADDIM).reshape(1, d_ssm)
    full = lambda shape: pl.BlockSpec(shape, lambda b, c: (0,) * len(shape))
    in_specs = [
        pl.BlockSpec((tc, d_ssm), lambda b, c: (b * nc + c, zb)),
        pl.BlockSpec((tc, d_ssm), lambda b, c: (b * nc + c, zb + 1)),
        pl.BlockSpec((tc, bc_dim), lambda b, c: (b * nc + c, bcb)),
        pl.BlockSpec((tc, _LANE), lambda b, c: (b * nc + c, 0)),
        full((conv_w.shape[0], xbc_dim)), full((1, xbc_dim)),
        full((1, _LANE)), full((1, _LANE)), full((1, d_ssm)), full((1, d_ssm)),
        full((_LANE, d_ssm)),
    ]
    return pl.pallas_call(
        functools.partial(_ssd_kernel, tc=tc, d_ssm=d_ssm, k_len=conv_w.shape[0]),
        grid=(bsz, nc),
        in_specs=in_specs,
        out_specs=pl.BlockSpec((tc, d_ssm), lambda b, c: (b * nc + c, 0)),
        out_shape=jax.ShapeDtypeStruct((bsz * lp, d_ssm), BF16),
        scratch_shapes=[pltpu.VMEM((tc + _SSD_HALO, xbc_dim), F32),
                        pltpu.VMEM((SSM_GROUPS, SSM_STATE, d_ssm // SSM_GROUPS), F32)],
        compiler_params=_params(("parallel", "arbitrary"), 32 << 20),
        name="mamba2_ssd",
    )(proj, proj, proj, dt_raw, conv_w.astype(F32), conv_b.reshape(1, xbc_dim).astype(F32),
      pad_h(dt_bias), pad_h(a_log), d_x, norm_g.reshape(1, d_ssm).astype(F32), jnp.asarray(expand))


def kernel(x, meta, emb_ln_g, emb_ln_b, w_in, lam_q1, lam_k1, lam_q2, lam_k2, att_subln_g, cf_dw_w, cf_dw_b, cf_ln_g, cf_ln_b, cf_pw_w, cf_pw_b, pool_w, pool_b, pool_scale, ssm_conv_w, ssm_conv_b, ssm_dt_bias, ssm_a_log, ssm_d, ssm_norm_g, w_out, ln1_g, ln1_b, ln2_g, ln2_b, ffn_w_gate, ffn_w_up, ffn_w_down, moe_router, moe_w_gate, moe_w_up, moe_w_down):
    bsz, seq, d_model = x.shape
    depth = w_in.shape[0]
    n_meta = meta.shape[0]
    d_att = d_conv = d_pool = d_ssm = d_model // 4
    xbc_dim = d_ssm + 2 * SSM_GROUPS * SSM_STATE
    n_heads = d_ssm // SSM_HEADDIM
    d_main = 3 * d_att + 2 * d_conv + d_pool + d_ssm + xbc_dim
    alpha = (2 * depth) ** 0.25

    seq_len = n_meta + seq
    lp = -(-seq_len // _SEQ_ALIGN) * _SEQ_ALIGN
    m = bsz * lp
    meta_b = jnp.broadcast_to(meta[None].astype(x.dtype), (bsz, n_meta, d_model))
    tokens = jnp.concatenate([meta_b, x, jnp.zeros((bsz, lp - seq_len, d_model), x.dtype)], axis=1)
    h, h16 = _layer_norm(tokens.reshape(m, d_model), emb_ln_g, emb_ln_b)

    for l in range(depth):
        w_l = w_in[l]
        proj = _matmul([h16], w_l[:, :d_main].astype(BF16), BF16)
        w_dt = jnp.pad(w_l[:, d_main:], ((0, 0), (0, _LANE - n_heads))).astype(BF16)
        dt_raw = _matmul([h16], w_dt, F32)
        lam_init = 0.8 - 0.6 * math.exp(-0.3 * l)

        y_att = _diff_attention(proj, bsz, lp, d_att, (lam_q1[l], lam_k1[l], lam_q2[l], lam_k2[l]),
                                att_subln_g[l], lam_init)
        y_conv = _conformer_conv(proj, bsz, lp, 3 * d_att, d_conv, cf_dw_w[l], cf_dw_b[l],
                                 cf_ln_g[l], cf_ln_b[l], cf_pw_w[l], cf_pw_b[l])
        y_pool = _multiscale_pool(proj, bsz, lp, 3 * d_att + 2 * d_conv, d_pool,
                                  pool_w[l], pool_b[l], pool_scale[l])
        y_ssm = _mamba2_ssd(proj, dt_raw, bsz, lp, 3 * d_att + 2 * d_conv + d_pool, d_ssm,
                            ssm_conv_w[l], ssm_conv_b[l], ssm_dt_bias[l], ssm_a_log[l], ssm_d[l],
                            ssm_norm_g[l])
        z = _matmul([y_att, y_conv, y_pool, y_ssm], w_out[l].astype(BF16), F32, residual=h, alpha=alpha)
        h, h16 = _layer_norm(z, ln1_g[l], ln1_b[l])

        if l % 2 == 0:
            e = l // 2
            act = _swiglu_up(h16, ffn_w_gate[e][None].astype(BF16), ffn_w_up[e][None].astype(BF16))
            w_down = ffn_w_down[e].astype(BF16)
        else:
            e = l // 2
            gates = _router_gates(h, moe_router[e])
            act = _swiglu_up(h16, moe_w_gate[e].astype(BF16), moe_w_up[e].astype(BF16), gates)
            w_down = moe_w_down[e].reshape(-1, d_model).astype(BF16)
        z = _matmul([act], w_down, F32, residual=h, alpha=alpha)
        h, h16 = _layer_norm(z, ln2_g[l], ln2_b[l])

    return h.reshape(bsz, lp, d_model)[:, n_meta:seq_len]
```

```python
import functools
import math

import numpy as np
import jax
import jax.numpy as jnp
from jax import lax
from jax.experimental import pallas as pl
from jax.experimental.pallas import tpu as pltpu

F32 = jnp.float32
BF16 = jnp.bfloat16
HIGHEST = lax.Precision.HIGHEST

N_META = 16
ATT_HEADS = 8
CONV_K = 31
POOL_WINDOWS = (2, 4, 8, 16)
SSM_HEADDIM = 64
SSM_GROUPS = 4
SSM_STATE = 128
SSM_CONV_K = 4
N_EXPERTS = 8
LN_EPS = 1e-5

_LANE = 128
_SUBLANE = 8
_VMEM_CAP = 60 * 1024 * 1024

_SEQ_ALIGN = 256
_TM_TARGET = 1536
_TS_TARGET = 768
_TK_ATT = 256
_SSD_CHUNK = 128
_LN_ROWS = 256
_NEG = -1e30


def _divisor_tile(n, target, align):
    best = align
    t = align
    while t <= min(n, target):
        if n % t == 0:
            best = t
        t += align
    assert n % best == 0, (n, target, align)
    return best


def _params(sem, vmem_bytes):
    return pltpu.CompilerParams(
        dimension_semantics=sem,
        vmem_limit_bytes=int(min(max(vmem_bytes, 32 * 1024 * 1024), _VMEM_CAP)))


def _nbytes(shape, dtype):
    return int(np.prod(shape)) * jnp.dtype(dtype).itemsize


def _ln_kernel(z_ref, g_ref, b_ref, hf_ref, hb_ref, *maybe_packed_ref):
    z = z_ref[...]
    mu = jnp.mean(z, axis=-1, keepdims=True)
    zc = z - mu
    var = jnp.mean(zc * zc, axis=-1, keepdims=True)
    y = zc * lax.rsqrt(var + LN_EPS) * g_ref[...] + b_ref[...]
    hf_ref[...] = y
    hb_ref[...] = y.astype(BF16)
    if maybe_packed_ref:
        half = y.shape[1] // 2
        maybe_packed_ref[0][...] = _pack_halves(y[:, :half], y[:, half:])


def _layer_norm(z, g, b, packed=False):
    m, d = z.shape
    tm = _divisor_tile(m, _LN_ROWS, _SUBLANE)
    row = pl.BlockSpec((tm, d), lambda i: (i, 0))
    vec = pl.BlockSpec((1, d), lambda i: (0, 0))
    out_specs = [row, row]
    out_shape = [jax.ShapeDtypeStruct((m, d), F32), jax.ShapeDtypeStruct((m, d), BF16)]
    if packed:
        out_specs.append(pl.BlockSpec((tm, d // 2), lambda i: (i, 0)))
        out_shape.append(jax.ShapeDtypeStruct((m, d // 2), jnp.uint32))
    return pl.pallas_call(
        _ln_kernel,
        grid=(m // tm,),
        in_specs=[row, vec, vec],
        out_specs=out_specs,
        out_shape=out_shape,
        compiler_params=_params(("parallel",), 10 * _nbytes((tm, d), F32)),
        name="layer_norm",
    )(z, g.reshape(1, d).astype(F32), b.reshape(1, d).astype(F32))


def _mm_kernel(*refs, n_in, alpha, has_res):
    x_refs = refs[:n_in]
    w_refs = refs[n_in:2 * n_in]
    o_ref = refs[-1]
    acc = None
    for x_ref, w_ref in zip(x_refs, w_refs):
        d = jnp.dot(x_ref[...], w_ref[...], preferred_element_type=F32)
        acc = d if acc is None else acc + d
    if has_res:
        acc = acc + alpha * refs[2 * n_in][...]
    o_ref[...] = acc.astype(o_ref.dtype)


def _matmul(xs, w, out_dtype, residual=None, alpha=1.0):
    n_in = len(xs)
    m, kp = xs[0].shape
    k_total, n = w.shape
    assert k_total == n_in * kp
    tm = _divisor_tile(m, _TM_TARGET if k_total <= 4096 else _TM_TARGET // 2, _LANE)
    tn = _divisor_tile(n, 512 if k_total <= 4096 else 256, _LANE)
    if k_total > 8192:
        tm = _divisor_tile(m, _TM_TARGET // 4, _LANE)
    in_specs = [pl.BlockSpec((tm, kp), lambda i, j: (i, 0)) for _ in range(n_in)]
    in_specs += [pl.BlockSpec((kp, tn), lambda i, j, p=p: (p, j)) for p in range(n_in)]
    args = list(xs) + [w] * n_in
    vmem = 2 * (n_in * _nbytes((tm, kp), BF16) + n_in * _nbytes((kp, tn), BF16)
                + _nbytes((tm, tn), out_dtype)) + 2 * _nbytes((tm, tn), F32)
    if residual is not None:
        in_specs.append(pl.BlockSpec((tm, tn), lambda i, j: (i, j)))
        args.append(residual)
        vmem += 2 * _nbytes((tm, tn), F32)
    return pl.pallas_call(
        functools.partial(_mm_kernel, n_in=n_in, alpha=alpha, has_res=residual is not None),
        grid=(m // tm, n // tn),
        in_specs=in_specs,
        out_specs=pl.BlockSpec((tm, tn), lambda i, j: (i, j)),
        out_shape=jax.ShapeDtypeStruct((m, n), out_dtype),
        compiler_params=_params(("parallel", "arbitrary"), vmem + (4 << 20)),
        name="matmul",
    )(*args)


def _glu_kernel(x_ref, wg_ref, wu_ref, o_ref):
    x = x_ref[...]
    g = jnp.dot(x, wg_ref[...], preferred_element_type=F32)
    u = jnp.dot(x, wu_ref[...], preferred_element_type=F32)
    o_ref[...] = (g * jax.nn.sigmoid(g) * u).astype(o_ref.dtype)


def _swiglu_up(x, wg, wu):
    m, d = x.shape
    f = wg.shape[1]
    tm = _divisor_tile(m, _TM_TARGET, _LANE)
    tn = _divisor_tile(f, 256, _LANE)
    w_spec = pl.BlockSpec((d, tn), lambda i, j: (0, j))
    vmem = 2 * (_nbytes((tm, d), BF16) + 2 * _nbytes((d, tn), BF16) + _nbytes((tm, tn), BF16)) \
        + 4 * _nbytes((tm, tn), F32)
    return pl.pallas_call(
        _glu_kernel,
        grid=(m // tm, f // tn),
        in_specs=[pl.BlockSpec((tm, d), lambda i, j: (i, 0)), w_spec, w_spec],
        out_specs=pl.BlockSpec((tm, tn), lambda i, j: (i, j)),
        out_shape=jax.ShapeDtypeStruct((m, f), BF16),
        compiler_params=_params(("parallel", "arbitrary"), vmem + (4 << 20)),
        name="swiglu_up",
    )(x, wg, wu)


def _router_kernel(h_ref, r_ref, o_ref, *, n_exp):
    logits = jnp.dot(h_ref[...], r_ref[...], preferred_element_type=F32, precision=HIGHEST)
    lane = lax.broadcasted_iota(jnp.int32, logits.shape, 1).astype(F32)
    lg = jnp.where(lane < n_exp, logits, _NEG)
    m1 = jnp.max(lg, axis=1, keepdims=True)
    i1 = jnp.min(jnp.where(lg == m1, lane, float(_LANE)), axis=1, keepdims=True)
    lg2 = jnp.where(lane == i1, _NEG, lg)
    m2 = jnp.max(lg2, axis=1, keepdims=True)
    i2 = jnp.min(jnp.where(lg2 == m2, lane, float(_LANE)), axis=1, keepdims=True)
    e = jnp.exp(m2 - m1)
    w1 = 1.0 / (1.0 + e)
    w2 = e / (1.0 + e)
    o_ref[...] = (jnp.where(lane == 0.0, w1, 0.0) + jnp.where(lane == 1.0, w2, 0.0)
                  + jnp.where(lane == 2.0, i1, 0.0) + jnp.where(lane == 3.0, i2, 0.0))


def _router_top2(h, router):
    m, d = h.shape
    n_exp = router.shape[1]
    tm = _divisor_tile(m, 512, _SUBLANE)
    r_pad = jnp.pad(router.astype(F32), ((0, 0), (0, _LANE - n_exp)))
    return pl.pallas_call(
        functools.partial(_router_kernel, n_exp=n_exp),
        grid=(m // tm,),
        in_specs=[pl.BlockSpec((tm, d), lambda i: (i, 0)), pl.BlockSpec((d, _LANE), lambda i: (0, 0))],
        out_specs=pl.BlockSpec((tm, _LANE), lambda i: (i, 0)),
        out_shape=jax.ShapeDtypeStruct((m, _LANE), F32),
        compiler_params=_params(("parallel",), 4 * _nbytes((tm, d), F32) + (8 << 20)),
        name="router",
    )(h, r_pad)


_HI16 = np.uint32(0xFFFF0000)
_MOE_ROWS = 512
_COMBINE_ROWS = 256


def _pack_halves(lo, hi):
    lo_bits = lax.bitcast_convert_type(lo.astype(BF16).astype(F32), jnp.uint32)
    hi_bits = lax.bitcast_convert_type(hi.astype(BF16).astype(F32), jnp.uint32)
    return (lo_bits >> 16) | (hi_bits & _HI16)


def _unpack_halves(words):
    lo = lax.bitcast_convert_type(words << 16, F32)
    hi = lax.bitcast_convert_type(words & _HI16, F32)
    return lo, hi


def _row_copy(src_hbm, row, dst_ref, r, sem):
    return pltpu.make_async_copy(src_hbm.at[pl.ds(row, 1)], dst_ref.at[pl.ds(r, 1)], sem)


def _gather_kernel(idx_ref, src_hbm, o_ref, sem, *, rows):
    def issue(r, carry):
        _row_copy(src_hbm, idx_ref[0, r], o_ref, r, sem).start()
        return carry

    def drain(r, carry):
        _row_copy(src_hbm, 0, o_ref, r, sem).wait()
        return carry

    lax.fori_loop(0, rows, issue, 0)
    lax.fori_loop(0, rows, drain, 0)


def _gather_rows(src, idx, rows):
    n_t = idx.shape[0] // rows
    w = src.shape[1]
    return pl.pallas_call(
        functools.partial(_gather_kernel, rows=rows),
        grid=(n_t,),
        in_specs=[pl.BlockSpec((None, 1, rows), lambda t: (t, 0, 0), memory_space=pltpu.SMEM),
                  pl.BlockSpec(memory_space=pl.ANY)],
        out_specs=pl.BlockSpec((rows, w), lambda t: (t, 0)),
        out_shape=jax.ShapeDtypeStruct((n_t * rows, w), src.dtype),
        scratch_shapes=[pltpu.SemaphoreType.DMA(())],
        compiler_params=_params(("arbitrary",), 4 * _nbytes((rows, w), src.dtype)),
        name="gather_rows",
    )(idx.reshape(n_t, 1, rows), src)


def _moe_up_kernel(te_ref, x_ref, wg_ref, wu_ref, o_ref, xb_ref):
    half = x_ref.shape[1]

    @pl.when(pl.program_id(1) == 0)
    def _():
        lo, hi = _unpack_halves(x_ref[...])
        xb_ref[:, :half] = lo.astype(BF16)
        xb_ref[:, half:] = hi.astype(BF16)

    x = xb_ref[...]
    g = jnp.dot(x, wg_ref[...], preferred_element_type=F32)
    u = jnp.dot(x, wu_ref[...], preferred_element_type=F32)
    o_ref[...] = (g * jax.nn.sigmoid(g) * u).astype(o_ref.dtype)


def _moe_up(xs, wg, wu, tile_expert):
    n_rows, half = xs.shape
    _, d, f = wg.shape
    rows = _MOE_ROWS
    tn = _divisor_tile(f, 512, _LANE)
    w_spec = pl.BlockSpec((None, d, tn), lambda t, j, te: (te[t], 0, j))
    vmem = 2 * (_nbytes((rows, half), jnp.uint32) + 2 * _nbytes((d, tn), BF16) + _nbytes((rows, tn), BF16)) \
        + _nbytes((rows, d), BF16) + 4 * _nbytes((rows, tn), F32)
    return pl.pallas_call(
        _moe_up_kernel,
        grid_spec=pltpu.PrefetchScalarGridSpec(
            num_scalar_prefetch=1,
            grid=(n_rows // rows, f // tn),
            in_specs=[pl.BlockSpec((rows, half), lambda t, j, te: (t, 0)), w_spec, w_spec],
            out_specs=pl.BlockSpec((rows, tn), lambda t, j, te: (t, j)),
            scratch_shapes=[pltpu.VMEM((rows, d), BF16)]),
        out_shape=jax.ShapeDtypeStruct((n_rows, f), BF16),
        compiler_params=_params(("arbitrary", "arbitrary"), vmem + (4 << 20)),
        name="moe_up",
    )(tile_expert, xs, wg, wu)


def _moe_down_kernel(te_ref, a_ref, wlo_ref, whi_ref, o_ref):
    a = a_ref[...]
    lo = jnp.dot(a, wlo_ref[...], preferred_element_type=F32)
    hi = jnp.dot(a, whi_ref[...], preferred_element_type=F32)
    o_ref[...] = _pack_halves(lo, hi)


def _moe_down(act, wd, tile_expert):
    n_rows, f = act.shape
    d = wd.shape[2]
    half = d // 2
    rows = _MOE_ROWS
    tn = _divisor_tile(half, 512, _LANE)
    nj = half // tn
    vmem = 2 * (_nbytes((rows, f), BF16) + 2 * _nbytes((f, tn), BF16) + _nbytes((rows, tn), jnp.uint32)) \
        + 4 * _nbytes((rows, tn), F32)
    return pl.pallas_call(
        _moe_down_kernel,
        grid_spec=pltpu.PrefetchScalarGridSpec(
            num_scalar_prefetch=1,
            grid=(n_rows // rows, nj),
            in_specs=[pl.BlockSpec((rows, f), lambda t, j, te: (t, 0)),
                      pl.BlockSpec((None, f, tn), lambda t, j, te: (te[t], 0, j)),
                      pl.BlockSpec((None, f, tn), lambda t, j, te: (te[t], 0, nj + j))],
            out_specs=pl.BlockSpec((rows, tn), lambda t, j, te: (t, j))),
        out_shape=jax.ShapeDtypeStruct((n_rows, half), jnp.uint32),
        compiler_params=_params(("arbitrary", "arbitrary"), vmem + (4 << 20)),
        name="moe_down",
    )(tile_expert, act, wd, wd)


def _moe_combine_kernel(i1_ref, i2_ref, y_hbm, r_ref, h_ref, g_ref, b_ref, hf_ref, hb_ref,
                        ya_ref, yb_ref, sem, *, rows, alpha):
    def issue(r, carry):
        _row_copy(y_hbm, i1_ref[0, r], ya_ref, r, sem).start()
        _row_copy(y_hbm, i2_ref[0, r], yb_ref, r, sem).start()
        return carry

    def drain(r, carry):
        _row_copy(y_hbm, 0, ya_ref, r, sem).wait()
        _row_copy(y_hbm, 0, yb_ref, r, sem).wait()
        return carry

    lax.fori_loop(0, rows, issue, 0)
    lax.fori_loop(0, rows, drain, 0)

    half = ya_ref.shape[1]
    w1 = r_ref[:, 0:1]
    w2 = r_ref[:, 1:2]
    a_lo, a_hi = _unpack_halves(ya_ref[...])
    b_lo, b_hi = _unpack_halves(yb_ref[...])
    z_lo = alpha * h_ref[:, :half] + (w1 * a_lo + w2 * b_lo)
    z_hi = alpha * h_ref[:, half:] + (w1 * a_hi + w2 * b_hi)
    inv_d = 1.0 / (2 * half)
    mu = (jnp.sum(z_lo, axis=-1, keepdims=True) + jnp.sum(z_hi, axis=-1, keepdims=True)) * inv_d
    c_lo = z_lo - mu
    c_hi = z_hi - mu
    var = (jnp.sum(c_lo * c_lo, axis=-1, keepdims=True) + jnp.sum(c_hi * c_hi, axis=-1, keepdims=True)) * inv_d
    rs = lax.rsqrt(var + LN_EPS)
    y_lo = c_lo * rs * g_ref[:, :half] + b_ref[:, :half]
    y_hi = c_hi * rs * g_ref[:, half:] + b_ref[:, half:]
    hf_ref[:, :half] = y_lo
    hf_ref[:, half:] = y_hi
    hb_ref[:, :half] = y_lo.astype(BF16)
    hb_ref[:, half:] = y_hi.astype(BF16)


def _moe_combine(ypk, dest1, dest2, top2, h, g, b, alpha):
    m, d = h.shape
    rows = _divisor_tile(m, _COMBINE_ROWS, _SUBLANE)
    n_t = m // rows
    idx_spec = pl.BlockSpec((None, 1, rows), lambda t: (t, 0, 0), memory_space=pltpu.SMEM)
    row = pl.BlockSpec((rows, d), lambda t: (t, 0))
    vec = pl.BlockSpec((1, d), lambda t: (0, 0))
    vmem = 2 * (_nbytes((rows, d), F32) * 2 + _nbytes((rows, d), BF16)) + 2 * _nbytes((rows, d // 2), F32) \
        + 6 * _nbytes((rows, d), F32)
    return pl.pallas_call(
        functools.partial(_moe_combine_kernel, rows=rows, alpha=alpha),
        grid=(n_t,),
        in_specs=[idx_spec, idx_spec, pl.BlockSpec(memory_space=pl.ANY),
                  pl.BlockSpec((rows, _LANE), lambda t: (t, 0)), row, vec, vec],
        out_specs=[row, row],
        out_shape=[jax.ShapeDtypeStruct((m, d), F32), jax.ShapeDtypeStruct((m, d), BF16)],
        scratch_shapes=[pltpu.VMEM((rows, d // 2), jnp.uint32), pltpu.VMEM((rows, d // 2), jnp.uint32),
                        pltpu.SemaphoreType.DMA(())],
        compiler_params=_params(("arbitrary",), vmem),
        name="moe_combine",
    )(dest1.reshape(n_t, 1, rows), dest2.reshape(n_t, 1, rows), ypk, top2, h,
      g.reshape(1, d).astype(F32), b.reshape(1, d).astype(F32))


def _moe_routing(top2, n_exp, rows):
    m = top2.shape[0]
    choice = top2[:, 2:4].astype(jnp.int32).reshape(-1)
    onehot = (choice[:, None] == jnp.arange(n_exp, dtype=jnp.int32)[None, :]).astype(jnp.int32)
    running = jnp.cumsum(onehot, axis=0)
    rank = jnp.sum(onehot * running, axis=1) - 1
    counts = running[-1]
    padded = ((counts + rows - 1) // rows) * rows
    ends = jnp.cumsum(padded)
    starts = ends - padded
    n_tiles = -(-2 * m // rows) + n_exp
    dest = jnp.clip(jnp.sum(onehot * starts[None, :], axis=1) + rank, 0, n_tiles * rows - 1)
    src_tok = jnp.zeros((n_tiles * rows,), jnp.int32).at[dest].set(
        jnp.arange(2 * m, dtype=jnp.int32) // 2)
    tile_start = jnp.arange(n_tiles, dtype=jnp.int32) * rows
    tile_expert = jnp.minimum(jnp.sum((tile_start[:, None] >= ends[None, :]).astype(jnp.int32), axis=1),
                              n_exp - 1)
    return src_tok, tile_expert, dest[0::2], dest[1::2]


def _attn_kernel(slopes_ref, q_ref, k_ref, v_ref, lq1_ref, lk1_ref, lq2_ref, lk2_ref, g_ref,
                 o_ref, qqt_ref, vt_ref, m_ref, l_ref, acc_ref, *, tq, tk, qk_dim, scale, lam_init):
    h = pl.program_id(1)
    i = pl.program_id(2)
    slope = slopes_ref[h]
    r = tq // tk
    lp = k_ref.shape[0]
    vdim = v_ref.shape[1]

    @pl.when(i == 0)
    def _():
        def xpose(c, carry):
            c0 = pl.multiple_of(c * tk, tk)
            vt_ref[:, pl.ds(c0, tk)] = v_ref[pl.ds(c0, tk), :].astype(F32).T.astype(BF16)
            return carry
        lax.fori_loop(0, lp // tk, xpose, 0)

    qs = q_ref[...].astype(F32) * scale
    lane = lax.broadcasted_iota(jnp.int32, qs.shape, 1)
    one0 = jnp.where(lane == 0, 1.0, 0.0).astype(F32)
    qa1 = jnp.concatenate([jnp.where(lane < qk_dim, qs, 0.0), one0], axis=1)
    qa2 = jnp.concatenate([jnp.where(lane >= qk_dim, qs, 0.0), one0], axis=1)
    qqt_ref[:, :tq] = qa1.T.astype(BF16)
    qqt_ref[:, tq:] = qa2.T.astype(BF16)
    kj_idx = lax.broadcasted_iota(jnp.int32, (tk, vdim), 0).astype(F32)
    kl_idx = lax.broadcasted_iota(jnp.int32, (tk, vdim), 1)
    kb = jnp.where(kl_idx == 0, slope * kj_idx, 0.0).astype(BF16)

    m_ref[...] = jnp.full(m_ref.shape, _NEG, F32)
    l_ref[...] = jnp.zeros(l_ref.shape, F32)
    acc_ref[...] = jnp.zeros(acc_ref.shape, F32)

    def step(j, masked):
        k0 = pl.multiple_of(j * tk, tk)
        ka = jnp.concatenate([k_ref[pl.ds(k0, tk), :], kb], axis=1)
        s = jnp.dot(ka, qqt_ref[...], preferred_element_type=F32)
        off = j * tk - i * tq
        if masked:
            krow = lax.broadcasted_iota(jnp.int32, s.shape, 0)
            qcol = lax.broadcasted_iota(jnp.int32, s.shape, 1)
            qcol = jnp.where(qcol >= tq, qcol - tq, qcol)
            s = jnp.where(krow - qcol <= -off, s, _NEG)
        c = slope * off.astype(F32)
        m_old = m_ref[...]
        m_new = jnp.maximum(m_old, jnp.max(s, axis=0, keepdims=True) + c)
        p = jnp.exp(s + (c - m_new))
        alpha = jnp.exp(m_old - m_new)
        l_ref[...] = alpha * l_ref[...] + jnp.sum(p, axis=0, keepdims=True)
        pv = jnp.dot(vt_ref[:, pl.ds(k0, tk)], p.astype(BF16), preferred_element_type=F32)
        acc_ref[...] = alpha * acc_ref[...] + pv
        m_ref[...] = m_new

    def body(j, carry):
        step(j, False)
        return carry

    lax.fori_loop(0, i * r, body, 0)
    for d in range(r):
        step(i * r + d, True)

    o12 = acc_ref[...] * (1.0 / l_ref[...])
    lam = (jnp.exp(jnp.sum(lq1_ref[...] * lk1_ref[...], axis=1, keepdims=True))
           - jnp.exp(jnp.sum(lq2_ref[...] * lk2_ref[...], axis=1, keepdims=True)) + lam_init)
    ot = o12[:, :tq] - lam * o12[:, tq:]
    ms = jnp.mean(ot * ot, axis=0, keepdims=True)
    y = ot * lax.rsqrt(ms + LN_EPS) * g_ref[...] * (1.0 - lam_init)
    o_ref[...] = y.T.astype(o_ref.dtype)


def _diff_attention(proj, bsz, lp, d_att, lam_vecs, subln_g, lam_init):
    vdim = d_att // ATT_HEADS
    qk_dim = vdim // 2
    assert vdim == _LANE
    tq = _divisor_tile(lp, _TS_TARGET, _SEQ_ALIGN)
    tk = _divisor_tile(tq, _TK_ATT, _LANE)
    nq = lp // tq
    slopes_np = 2.0 ** (-8.0 * np.arange(1, ATT_HEADS + 1) / ATT_HEADS)
    in_tile = np.outer(slopes_np, np.arange(tk)).astype(np.float32)
    assert np.array_equal(in_tile.astype(BF16).astype(np.float32), in_tile)
    slopes = jnp.asarray(slopes_np, F32)
    lq1, lk1, lq2, lk2 = [v.reshape(1, qk_dim).astype(F32) for v in lam_vecs]
    vec = pl.BlockSpec((1, qk_dim), lambda b, h, i: (0, 0))
    in_specs = [
        pl.BlockSpec(memory_space=pltpu.SMEM),
        pl.BlockSpec((tq, vdim), lambda b, h, i: (b * nq + i, h)),
        pl.BlockSpec((lp, vdim), lambda b, h, i: (b, ATT_HEADS + h)),
        pl.BlockSpec((lp, vdim), lambda b, h, i: (b, 2 * ATT_HEADS + h)),
        vec, vec, vec, vec,
        pl.BlockSpec((vdim, 1), lambda b, h, i: (0, 0)),
    ]
    vmem = 2 * (2 * _nbytes((lp, vdim), BF16) + 2 * _nbytes((tq, vdim), BF16)) \
        + _nbytes((lp, vdim), BF16) + 4 * _nbytes((2 * tq, vdim), F32) + 6 * _nbytes((2 * tq, tk), F32)
    return pl.pallas_call(
        functools.partial(_attn_kernel, tq=tq, tk=tk, qk_dim=qk_dim, scale=qk_dim ** -0.5,
                          lam_init=lam_init),
        grid=(bsz, ATT_HEADS, nq),
        in_specs=in_specs,
        out_specs=pl.BlockSpec((tq, vdim), lambda b, h, i: (b * nq + i, h)),
        out_shape=jax.ShapeDtypeStruct((bsz * lp, d_att), BF16),
        scratch_shapes=[pltpu.VMEM((2 * vdim, 2 * tq), BF16), pltpu.VMEM((vdim, lp), BF16),
                        pltpu.VMEM((1, 2 * tq), F32), pltpu.VMEM((1, 2 * tq), F32),
                        pltpu.VMEM((vdim, 2 * tq), F32)],
        compiler_params=_params(("parallel", "parallel", "arbitrary"), vmem + (8 << 20)),
        name="diff_attention",
    )(slopes, proj, proj, proj, lq1, lk1, lq2, lk2, subln_g.reshape(vdim, 1).astype(F32))


_CONV_HALO = 32
_CONV_ROWS = 32


def _conv_kernel(a_ref, gate_ref, dw_ref, dwb_ref, lng_ref, lnb_ref, pw_ref, pwb_ref, o_ref,
                 hbuf_ref, cbuf_ref, sh_ref, *, ts, k_len):
    t = pl.program_id(1)
    c_dim = a_ref.shape[1]

    @pl.when(t == 0)
    def _():
        hbuf_ref[0:_CONV_HALO, :] = jnp.zeros((_CONV_HALO, c_dim), F32)

    @pl.when(t > 0)
    def _():
        hbuf_ref[0:_CONV_HALO, :] = hbuf_ref[ts:ts + _CONV_HALO, :]

    a = a_ref[...].astype(F32)
    hbuf_ref[_CONV_HALO:_CONV_HALO + ts, :] = a * jax.nn.sigmoid(gate_ref[...].astype(F32))

    first = _CONV_HALO - (k_len - 1)
    sh_rows = sh_ref.shape[1]
    for lb in range(c_dim // _LANE):
        lanes = slice(lb * _LANE, (lb + 1) * _LANE)
        for b in range(1, _SUBLANE):
            sh_ref[b - 1] = hbuf_ref[b:b + sh_rows, lanes]
        taps = [dw_ref[k:k + 1, lanes] for k in range(k_len)]
        bias = dwb_ref[:, lanes]

        def chunk(c, carry, lanes=lanes, taps=taps, bias=bias):
            base = c * _CONV_ROWS
            acc = jnp.broadcast_to(bias, (_CONV_ROWS, _LANE))
            for k in range(k_len):
                a, b = divmod(first + k, _SUBLANE)
                start = pl.multiple_of(base + a * _SUBLANE, _SUBLANE)
                if b == 0:
                    src = hbuf_ref[pl.ds(start, _CONV_ROWS), lanes]
                else:
                    src = sh_ref[b - 1, pl.ds(start, _CONV_ROWS), :]
                acc = acc + taps[k] * src
            cbuf_ref[pl.ds(pl.multiple_of(base, _CONV_ROWS), _CONV_ROWS), lanes] = acc
            return carry

        lax.fori_loop(0, ts // _CONV_ROWS, chunk, 0)

    y = cbuf_ref[...]
    mu = jnp.mean(y, axis=-1, keepdims=True)
    yc = y - mu
    var = jnp.mean(yc * yc, axis=-1, keepdims=True)
    yn = yc * lax.rsqrt(var + LN_EPS) * lng_ref[...] + lnb_ref[...]
    act = (yn * jax.nn.sigmoid(yn)).astype(BF16)
    out = jnp.dot(act, pw_ref[...], preferred_element_type=F32) + pwb_ref[...]
    o_ref[...] = out.astype(o_ref.dtype)


def _conformer_conv(proj, bsz, lp, col0, d_conv, dw_w, dw_b, ln_g, ln_b, pw_w, pw_b):
    ts = _divisor_tile(lp, _TS_TARGET, _CONV_ROWS)
    nt = lp // ts
    cb = col0 // d_conv
    assert col0 % d_conv == 0 and dw_w.shape[0] - 1 <= _CONV_HALO
    vec = pl.BlockSpec((1, d_conv), lambda b, t: (0, 0))
    in_specs = [
        pl.BlockSpec((ts, d_conv), lambda b, t: (b * nt + t, cb)),
        pl.BlockSpec((ts, d_conv), lambda b, t: (b * nt + t, cb + 1)),
        pl.BlockSpec((dw_w.shape[0], d_conv), lambda b, t: (0, 0)),
        vec, vec, vec,
        pl.BlockSpec((d_conv, d_conv), lambda b, t: (0, 0)),
        vec,
    ]
    vmem = 4 * _nbytes((ts, d_conv), BF16) * 2 + 8 * _nbytes((ts + _CONV_HALO, d_conv), F32)
    r1 = lambda v: v.reshape(1, d_conv).astype(F32)
    return pl.pallas_call(
        functools.partial(_conv_kernel, ts=ts, k_len=dw_w.shape[0]),
        grid=(bsz, nt),
        in_specs=in_specs,
        out_specs=pl.BlockSpec((ts, d_conv), lambda b, t: (b * nt + t, 0)),
        out_shape=jax.ShapeDtypeStruct((bsz * lp, d_conv), BF16),
        scratch_shapes=[pltpu.VMEM((ts + _CONV_HALO, d_conv), F32), pltpu.VMEM((ts, d_conv), F32),
                        pltpu.VMEM((_SUBLANE - 1, ts + _CONV_HALO - _SUBLANE, _LANE), F32)],
        compiler_params=_params(("parallel", "arbitrary"), vmem),
        name="conformer_conv",
    )(proj, proj, dw_w.astype(F32), r1(dw_b), r1(ln_g), r1(ln_b), pw_w.astype(BF16), r1(pw_b))


_POOL_HALO = 16


def _pool_kernel(u_ref, w_ref, b_ref, s_ref, o_ref, buf_ref, *, ts, windows, gw):
    t = pl.program_id(1)
    c_dim = u_ref.shape[1]

    @pl.when(t == 0)
    def _():
        buf_ref[0:_POOL_HALO, :] = jnp.zeros((_POOL_HALO, c_dim), F32)

    @pl.when(t > 0)
    def _():
        buf_ref[0:_POOL_HALO, :] = buf_ref[ts:ts + _POOL_HALO, :]

    buf_ref[_POOL_HALO:_POOL_HALO + ts, :] = u_ref[...].astype(F32)
    pos = (t * ts + 1 + lax.broadcasted_iota(jnp.int32, (ts, 1), 0)).astype(F32)
    for gi, win in enumerate(windows):
        lanes = slice(gi * gw, (gi + 1) * gw)
        u = buf_ref[_POOL_HALO:_POOL_HALO + ts, lanes]
        tot = u
        for j in range(1, win):
            tot = tot + buf_ref[_POOL_HALO - j:_POOL_HALO - j + ts, lanes]
        d = tot / jnp.minimum(pos, float(win)) - u
        y = jnp.dot(d.astype(BF16), w_ref[gi], preferred_element_type=F32)
        o_ref[:, lanes] = ((y + b_ref[:, lanes]) * s_ref[:, lanes]).astype(o_ref.dtype)


def _multiscale_pool(proj, bsz, lp, col0, d_pool, w, bias, scale):
    n_g = len(POOL_WINDOWS)
    gw = d_pool // n_g
    ts = _divisor_tile(lp, _TS_TARGET, _SUBLANE)
    nt = lp // ts
    assert col0 % d_pool == 0 and max(POOL_WINDOWS) - 1 <= _POOL_HALO
    cb = col0 // d_pool
    vec = pl.BlockSpec((1, d_pool), lambda b, t: (0, 0))
    r1 = lambda v: v.reshape(1, d_pool).astype(F32)
    return pl.pallas_call(
        functools.partial(_pool_kernel, ts=ts, windows=POOL_WINDOWS, gw=gw),
        grid=(bsz, nt),
        in_specs=[pl.BlockSpec((ts, d_pool), lambda b, t: (b * nt + t, cb)),
                  pl.BlockSpec((n_g, gw, gw), lambda b, t: (0, 0, 0)), vec, vec],
        out_specs=pl.BlockSpec((ts, d_pool), lambda b, t: (b * nt + t, 0)),
        out_shape=jax.ShapeDtypeStruct((bsz * lp, d_pool), BF16),
        scratch_shapes=[pltpu.VMEM((ts + _POOL_HALO, d_pool), F32)],
        compiler_params=_params(("parallel", "arbitrary"), 10 * _nbytes((ts, d_pool), F32)),
        name="multiscale_pool",
    )(proj, w.astype(BF16), r1(bias), r1(scale))


_SSD_HALO = 8


def _ssd_kernel(z_ref, x_ref, bc_ref, dt_ref, cw_ref, cb_ref, dtb_ref, alog_ref, dskip_ref, ng_ref,
                expand_ref, o_ref, ext_ref, state_ref, *, tc, d_ssm, k_len):
    c = pl.program_id(1)
    n_state = SSM_STATE
    gwid = d_ssm // SSM_GROUPS
    hpg = gwid // SSM_HEADDIM
    xbc_dim = x_ref.shape[1] + bc_ref.shape[1]

    @pl.when(c == 0)
    def _():
        ext_ref[0:_SSD_HALO, :] = jnp.zeros((_SSD_HALO, xbc_dim), F32)
        state_ref[...] = jnp.zeros(state_ref.shape, F32)

    @pl.when(c > 0)
    def _():
        ext_ref[0:_SSD_HALO, :] = ext_ref[tc:tc + _SSD_HALO, :]

    ext_ref[_SSD_HALO:_SSD_HALO + tc, :d_ssm] = x_ref[...].astype(F32)
    ext_ref[_SSD_HALO:_SSD_HALO + tc, d_ssm:] = bc_ref[...].astype(F32)

    first = _SSD_HALO - (k_len - 1)
    pre = jnp.broadcast_to(cb_ref[...], (tc, xbc_dim))
    for k in range(k_len):
        pre = pre + cw_ref[k:k + 1, :] * ext_ref[first + k:first + k + tc, :]
    xbc = pre * jax.nn.sigmoid(pre)
    xs = xbc[:, :d_ssm]
    bm = xbc[:, d_ssm:d_ssm + SSM_GROUPS * n_state].astype(BF16)
    cm = xbc[:, d_ssm + SSM_GROUPS * n_state:].astype(BF16)

    dt = jax.nn.softplus(dt_ref[...] + dtb_ref[...])
    a = dt * (-jnp.exp(alog_ref[...]))
    ti = lax.broadcasted_iota(jnp.int32, (tc, tc), 0)
    si = lax.broadcasted_iota(jnp.int32, (tc, tc), 1)
    causal = ti >= si
    tri = jnp.where(causal, 1.0, 0.0).astype(F32)
    a_cs = jnp.dot(tri, a, preferred_element_type=F32, precision=HIGHEST)
    a_cs_t = a_cs.T
    expand = expand_ref[...]
    a_cs_x = jnp.dot(a_cs, expand, preferred_element_type=F32, precision=HIGHEST)
    dt_x = jnp.dot(dt, expand, preferred_element_type=F32, precision=HIGHEST)
    a_last = a_cs_x[tc - 1:tc, :]
    x_dt = xs * dt_x
    x_end = (x_dt * jnp.exp(a_last - a_cs_x)).astype(BF16)
    x_dt_b = x_dt.astype(BF16)
    decay_in = jnp.exp(a_cs_x)
    decay_chunk = jnp.exp(a_last)

    lane_g = lax.broadcasted_iota(jnp.int32, (tc, gwid), 1)
    ys = []
    for g in range(SSM_GROUPS):
        cg = cm[:, g * n_state:(g + 1) * n_state]
        bg = bm[:, g * n_state:(g + 1) * n_state]
        gl = slice(g * gwid, (g + 1) * gwid)
        cb = lax.dot_general(cg, bg, (((1,), (1,)), ((), ())), preferred_element_type=F32)
        xg = x_dt_b[:, gl]
        y_g = None
        for r in range(hpg):
            hd = g * hpg + r
            seg = a_cs[:, hd:hd + 1] - a_cs_t[hd:hd + 1, :]
            lmat = jnp.exp(jnp.where(causal, seg, _NEG))
            mh = (cb * lmat).astype(BF16)
            in_head = (lane_g >= r * SSM_HEADDIM) & (lane_g < (r + 1) * SSM_HEADDIM)
            xh = jnp.where(in_head, xg, jnp.zeros_like(xg))
            yh = jnp.dot(mh, xh, preferred_element_type=F32)
            y_g = yh if y_g is None else y_g + yh
        st = state_ref[g]
        y_g = y_g + jnp.dot(cg, st.astype(BF16), preferred_element_type=F32) * decay_in[:, gl]
        upd = lax.dot_general(bg, x_end[:, gl], (((0,), (0,)), ((), ())), preferred_element_type=F32)
        state_ref[g] = decay_chunk[:, gl] * st + upd
        ys.append(y_g)
    y = jnp.concatenate(ys, axis=1) + dskip_ref[...] * xs
    zf = z_ref[...].astype(F32)
    y = y * (zf * jax.nn.sigmoid(zf))
    outs = []
    for g in range(SSM_GROUPS):
        yg = y[:, g * gwid:(g + 1) * gwid]
        ms = jnp.mean(yg * yg, axis=-1, keepdims=True)
        outs.append(yg * lax.rsqrt(ms + LN_EPS))
    o_ref[...] = (jnp.concatenate(outs, axis=1) * ng_ref[...]).astype(o_ref.dtype)


def _mamba2_ssd(proj, dt_raw, bsz, lp, z_col0, d_ssm, conv_w, conv_b, dt_bias, a_log, d_skip, norm_g):
    xbc_dim = d_ssm + 2 * SSM_GROUPS * SSM_STATE
    n_heads = d_ssm // SSM_HEADDIM
    tc = _SSD_CHUNK
    nc = lp // tc
    bc_dim = xbc_dim - d_ssm
    assert lp % tc == 0 and z_col0 % d_ssm == 0 and (z_col0 + 2 * d_ssm) % bc_dim == 0
    zb = z_col0 // d_ssm
    bcb = (z_col0 + 2 * d_ssm) // bc_dim
    pad_h = lambda v: jnp.pad(v.astype(F32).reshape(1, n_heads), ((0, 0), (0, _LANE - n_heads)))
    expand = (np.arange(d_ssm)[None, :] // SSM_HEADDIM == np.arange(_LANE)[:, None]).astype(np.float32)
    d_x = jnp.repeat(d_skip.astype(F32), SSM_HEADDIM).reshape(1, d_ssm)
    full = lambda shape: pl.BlockSpec(shape, lambda b, c: (0,) * len(shape))
    in_specs = [
        pl.BlockSpec((tc, d_ssm), lambda b, c: (b * nc + c, zb)),
        pl.BlockSpec((tc, d_ssm), lambda b, c: (b * nc + c, zb + 1)),
        pl.BlockSpec((tc, bc_dim), lambda b, c: (b * nc + c, bcb)),
        pl.BlockSpec((tc, _LANE), lambda b, c: (b * nc + c, 0)),
        full((conv_w.shape[0], xbc_dim)), full((1, xbc_dim)),
        full((1, _LANE)), full((1, _LANE)), full((1, d_ssm)), full((1, d_ssm)),
        full((_LANE, d_ssm)),
    ]
    return pl.pallas_call(
        functools.partial(_ssd_kernel, tc=tc, d_ssm=d_ssm, k_len=conv_w.shape[0]),
        grid=(bsz, nc),
        in_specs=in_specs,
        out_specs=pl.BlockSpec((tc, d_ssm), lambda b, c: (b * nc + c, 0)),
        out_shape=jax.ShapeDtypeStruct((bsz * lp, d_ssm), BF16),
        scratch_shapes=[pltpu.VMEM((tc + _SSD_HALO, xbc_dim), F32),
                        pltpu.VMEM((SSM_GROUPS, SSM_STATE, d_ssm // SSM_GROUPS), F32)],
        compiler_params=_params(("parallel", "arbitrary"), 32 << 20),
        name="mamba2_ssd",
    )(proj, proj, proj, dt_raw, conv_w.astype(F32), conv_b.reshape(1, xbc_dim).astype(F32),
      pad_h(dt_bias), pad_h(a_log), d_x, norm_g.reshape(1, d_ssm).astype(F32), jnp.asarray(expand))


def kernel(x, meta, emb_ln_g, emb_ln_b, w_in, lam_q1, lam_k1, lam_q2, lam_k2, att_subln_g, cf_dw_w, cf_dw_b, cf_ln_g, cf_ln_b, cf_pw_w, cf_pw_b, pool_w, pool_b, pool_scale, ssm_conv_w, ssm_conv_b, ssm_dt_bias, ssm_a_log, ssm_d, ssm_norm_g, w_out, ln1_g, ln1_b, ln2_g, ln2_b, ffn_w_gate, ffn_w_up, ffn_w_down, moe_router, moe_w_gate, moe_w_up, moe_w_down):
    bsz, seq, d_model = x.shape
    depth = w_in.shape[0]
    n_meta = meta.shape[0]
    d_att = d_conv = d_pool = d_ssm = d_model // 4
    xbc_dim = d_ssm + 2 * SSM_GROUPS * SSM_STATE
    n_heads = d_ssm // SSM_HEADDIM
    d_main = 3 * d_att + 2 * d_conv + d_pool + d_ssm + xbc_dim
    alpha = (2 * depth) ** 0.25

    seq_len = n_meta + seq
    lp = -(-seq_len // _SEQ_ALIGN) * _SEQ_ALIGN
    m = bsz * lp
    meta_b = jnp.broadcast_to(meta[None].astype(x.dtype), (bsz, n_meta, d_model))
    tokens = jnp.concatenate([meta_b, x, jnp.zeros((bsz, lp - seq_len, d_model), x.dtype)], axis=1)
    h, h16 = _layer_norm(tokens.reshape(m, d_model), emb_ln_g, emb_ln_b)

    for l in range(depth):
        w_l = w_in[l]
        proj = _matmul([h16], w_l[:, :d_main].astype(BF16), BF16)
        w_dt = jnp.pad(w_l[:, d_main:], ((0, 0), (0, _LANE - n_heads))).astype(BF16)
        dt_raw = _matmul([h16], w_dt, F32)
        lam_init = 0.8 - 0.6 * math.exp(-0.3 * l)

        y_att = _diff_attention(proj, bsz, lp, d_att, (lam_q1[l], lam_k1[l], lam_q2[l], lam_k2[l]),
                                att_subln_g[l], lam_init)
        y_conv = _conformer_conv(proj, bsz, lp, 3 * d_att, d_conv, cf_dw_w[l], cf_dw_b[l],
                                 cf_ln_g[l], cf_ln_b[l], cf_pw_w[l], cf_pw_b[l])
        y_pool = _multiscale_pool(proj, bsz, lp, 3 * d_att + 2 * d_conv, d_pool,
                                  pool_w[l], pool_b[l], pool_scale[l])
        y_ssm = _mamba2_ssd(proj, dt_raw, bsz, lp, 3 * d_att + 2 * d_conv + d_pool, d_ssm,
                            ssm_conv_w[l], ssm_conv_b[l], ssm_dt_bias[l], ssm_a_log[l], ssm_d[l],
                            ssm_norm_g[l])
        z = _matmul([y_att, y_conv, y_pool, y_ssm], w_out[l].astype(BF16), F32, residual=h, alpha=alpha)

        e = l // 2
        if l % 2 == 0:
            h, h16 = _layer_norm(z, ln1_g[l], ln1_b[l])
            act = _swiglu_up(h16, ffn_w_gate[e].astype(BF16), ffn_w_up[e].astype(BF16))
            z = _matmul([act], ffn_w_down[e].astype(BF16), F32, residual=h, alpha=alpha)
            h, h16 = _layer_norm(z, ln2_g[l], ln2_b[l])
        else:
            h, h16, h_packed = _layer_norm(z, ln1_g[l], ln1_b[l], packed=True)
            top2 = _router_top2(h, moe_router[e])
            src_tok, tile_expert, dest1, dest2 = _moe_routing(top2, moe_router.shape[2], _MOE_ROWS)
            xs = _gather_rows(h_packed, src_tok, _MOE_ROWS)
            act = _moe_up(xs, moe_w_gate[e].astype(BF16), moe_w_up[e].astype(BF16), tile_expert)
            y_sorted = _moe_down(act, moe_w_down[e].astype(BF16), tile_expert)
            h, h16 = _moe_combine(y_sorted, dest1, dest2, top2, h, ln2_g[l], ln2_b[l], alpha)

    return h.reshape(bsz, lp, d_model)[:, n_meta:seq_len]
```

```python
import functools
import math

import numpy as np
import jax
import jax.numpy as jnp
from jax import lax
from jax.experimental import pallas as pl
from jax.experimental.pallas import tpu as pltpu

F32 = jnp.float32
BF16 = jnp.bfloat16
HIGHEST = lax.Precision.HIGHEST

N_META = 16
ATT_HEADS = 8
CONV_K = 31
POOL_WINDOWS = (2, 4, 8, 16)
SSM_HEADDIM = 64
SSM_GROUPS = 4
SSM_STATE = 128
SSM_CONV_K = 4
N_EXPERTS = 8
LN_EPS = 1e-5

_LANE = 128
_SUBLANE = 8
_VMEM_CAP = 60 * 1024 * 1024

_SEQ_ALIGN = 256
_TM_TARGET = 1536
_TS_TARGET = 768
_TK_ATT = 256
_SSD_CHUNK = 128
_LN_ROWS = 256
_NEG = -1e30


def _divisor_tile(n, target, align):
    best = align
    t = align
    while t <= min(n, target):
        if n % t == 0:
            best = t
        t += align
    assert n % best == 0, (n, target, align)
    return best


def _params(sem, vmem_bytes):
    return pltpu.CompilerParams(
        dimension_semantics=sem,
        vmem_limit_bytes=int(min(max(vmem_bytes, 32 * 1024 * 1024), _VMEM_CAP)))


def _nbytes(shape, dtype):
    return int(np.prod(shape)) * jnp.dtype(dtype).itemsize


def _ln_kernel(z_ref, g_ref, b_ref, hf_ref, hb_ref, *maybe_packed_ref):
    z = z_ref[...]
    mu = jnp.mean(z, axis=-1, keepdims=True)
    zc = z - mu
    var = jnp.mean(zc * zc, axis=-1, keepdims=True)
    y = zc * lax.rsqrt(var + LN_EPS) * g_ref[...] + b_ref[...]
    hf_ref[...] = y
    hb_ref[...] = y.astype(BF16)
    if maybe_packed_ref:
        half = y.shape[1] // 2
        maybe_packed_ref[0][...] = _pack_halves(y[:, :half], y[:, half:])


def _layer_norm(z, g, b, packed=False):
    m, d = z.shape
    tm = _divisor_tile(m, _LN_ROWS, _SUBLANE)
    row = pl.BlockSpec((tm, d), lambda i: (i, 0))
    vec = pl.BlockSpec((1, d), lambda i: (0, 0))
    out_specs = [row, row]
    out_shape = [jax.ShapeDtypeStruct((m, d), F32), jax.ShapeDtypeStruct((m, d), BF16)]
    if packed:
        out_specs.append(pl.BlockSpec((tm, d // 2), lambda i: (i, 0)))
        out_shape.append(jax.ShapeDtypeStruct((m, d // 2), jnp.uint32))
    return pl.pallas_call(
        _ln_kernel,
        grid=(m // tm,),
        in_specs=[row, vec, vec],
        out_specs=out_specs,
        out_shape=out_shape,
        compiler_params=_params(("parallel",), 10 * _nbytes((tm, d), F32)),
        name="layer_norm",
    )(z, g.reshape(1, d).astype(F32), b.reshape(1, d).astype(F32))


def _cast_kernel(w_ref, o_ref):
    o_ref[...] = w_ref[...].astype(o_ref.dtype)


def _weight_bf16(w, s0, n_s, n_cols=None):
    _, r, c = w.shape
    n_cols = c if n_cols is None else n_cols
    rb = _divisor_tile(r, 512, _SUBLANE)
    cb = _divisor_tile(n_cols, 2048, _LANE)
    return pl.pallas_call(
        _cast_kernel,
        grid=(n_s, r // rb, n_cols // cb),
        in_specs=[pl.BlockSpec((None, rb, cb), lambda s, i, j: (s0 + s, i, j))],
        out_specs=pl.BlockSpec((None, rb, cb), lambda s, i, j: (s, i, j)),
        out_shape=jax.ShapeDtypeStruct((n_s, r, n_cols), BF16),
        compiler_params=_params(("parallel", "parallel", "parallel"), 4 * _nbytes((rb, cb), F32)),
        name="weight_bf16",
    )(w)


def _mm_kernel(*refs, n_in, alpha, has_res):
    x_refs = refs[:n_in]
    w_refs = refs[n_in:2 * n_in]
    o_ref = refs[-1]
    acc = None
    for x_ref, w_ref in zip(x_refs, w_refs):
        d = jnp.dot(x_ref[...], w_ref[...], preferred_element_type=F32)
        acc = d if acc is None else acc + d
    if has_res:
        acc = acc + alpha * refs[2 * n_in][...]
    o_ref[...] = acc.astype(o_ref.dtype)


def _matmul(xs, w, out_dtype, residual=None, alpha=1.0):
    n_in = len(xs)
    m, kp = xs[0].shape
    k_total, n = w.shape
    assert k_total == n_in * kp
    tm = _divisor_tile(m, _TM_TARGET if k_total <= 4096 else _TM_TARGET // 2, _LANE)
    tn = _divisor_tile(n, 512 if k_total <= 4096 else 256, _LANE)
    if k_total > 8192:
        tm = _divisor_tile(m, _TM_TARGET // 4, _LANE)
    in_specs = [pl.BlockSpec((tm, kp), lambda i, j: (i, 0)) for _ in range(n_in)]
    in_specs += [pl.BlockSpec((kp, tn), lambda i, j, p=p: (p, j)) for p in range(n_in)]
    args = list(xs) + [w] * n_in
    vmem = 2 * (n_in * _nbytes((tm, kp), BF16) + n_in * _nbytes((kp, tn), BF16)
                + _nbytes((tm, tn), out_dtype)) + 2 * _nbytes((tm, tn), F32)
    if residual is not None:
        in_specs.append(pl.BlockSpec((tm, tn), lambda i, j: (i, j)))
        args.append(residual)
        vmem += 2 * _nbytes((tm, tn), F32)
    return pl.pallas_call(
        functools.partial(_mm_kernel, n_in=n_in, alpha=alpha, has_res=residual is not None),
        grid=(m // tm, n // tn),
        in_specs=in_specs,
        out_specs=pl.BlockSpec((tm, tn), lambda i, j: (i, j)),
        out_shape=jax.ShapeDtypeStruct((m, n), out_dtype),
        compiler_params=_params(("parallel", "arbitrary"), vmem + (4 << 20)),
        name="matmul",
    )(*args)


def _glu_kernel(x_ref, wg_ref, wu_ref, o_ref):
    x = x_ref[...]
    g = jnp.dot(x, wg_ref[...], preferred_element_type=F32)
    u = jnp.dot(x, wu_ref[...], preferred_element_type=F32)
    o_ref[...] = (g * jax.nn.sigmoid(g) * u).astype(o_ref.dtype)


def _swiglu_up(x, wg, wu):
    m, d = x.shape
    f = wg.shape[1]
    tm = _divisor_tile(m, _TM_TARGET, _LANE)
    tn = _divisor_tile(f, 256, _LANE)
    w_spec = pl.BlockSpec((d, tn), lambda i, j: (0, j))
    vmem = 2 * (_nbytes((tm, d), BF16) + 2 * _nbytes((d, tn), BF16) + _nbytes((tm, tn), BF16)) \
        + 4 * _nbytes((tm, tn), F32)
    return pl.pallas_call(
        _glu_kernel,
        grid=(m // tm, f // tn),
        in_specs=[pl.BlockSpec((tm, d), lambda i, j: (i, 0)), w_spec, w_spec],
        out_specs=pl.BlockSpec((tm, tn), lambda i, j: (i, j)),
        out_shape=jax.ShapeDtypeStruct((m, f), BF16),
        compiler_params=_params(("parallel", "arbitrary"), vmem + (4 << 20)),
        name="swiglu_up",
    )(x, wg, wu)


def _router_kernel(h_ref, r_ref, o_ref, *, n_exp):
    logits = jnp.dot(h_ref[...], r_ref[...], preferred_element_type=F32, precision=HIGHEST)
    lane = lax.broadcasted_iota(jnp.int32, logits.shape, 1).astype(F32)
    lg = jnp.where(lane < n_exp, logits, _NEG)
    m1 = jnp.max(lg, axis=1, keepdims=True)
    i1 = jnp.min(jnp.where(lg == m1, lane, float(_LANE)), axis=1, keepdims=True)
    lg2 = jnp.where(lane == i1, _NEG, lg)
    m2 = jnp.max(lg2, axis=1, keepdims=True)
    i2 = jnp.min(jnp.where(lg2 == m2, lane, float(_LANE)), axis=1, keepdims=True)
    e = jnp.exp(m2 - m1)
    w1 = 1.0 / (1.0 + e)
    w2 = e / (1.0 + e)
    o_ref[...] = (jnp.where(lane == 0.0, w1, 0.0) + jnp.where(lane == 1.0, w2, 0.0)
                  + jnp.where(lane == 2.0, i1, 0.0) + jnp.where(lane == 3.0, i2, 0.0))


def _router_top2(h, router):
    m, d = h.shape
    n_exp = router.shape[1]
    tm = _divisor_tile(m, 512, _SUBLANE)
    r_pad = jnp.pad(router.astype(F32), ((0, 0), (0, _LANE - n_exp)))
    return pl.pallas_call(
        functools.partial(_router_kernel, n_exp=n_exp),
        grid=(m // tm,),
        in_specs=[pl.BlockSpec((tm, d), lambda i: (i, 0)), pl.BlockSpec((d, _LANE), lambda i: (0, 0))],
        out_specs=pl.BlockSpec((tm, _LANE), lambda i: (i, 0)),
        out_shape=jax.ShapeDtypeStruct((m, _LANE), F32),
        compiler_params=_params(("parallel",), 4 * _nbytes((tm, d), F32) + (8 << 20)),
        name="router",
    )(h, r_pad)


_HI16 = np.uint32(0xFFFF0000)
_MOE_ROWS = 512
_COMBINE_ROWS = 256


def _pack_halves(lo, hi):
    lo_bits = lax.bitcast_convert_type(lo.astype(BF16).astype(F32), jnp.uint32)
    hi_bits = lax.bitcast_convert_type(hi.astype(BF16).astype(F32), jnp.uint32)
    return (lo_bits >> 16) | (hi_bits & _HI16)


def _unpack_halves(words):
    lo = lax.bitcast_convert_type(words << 16, F32)
    hi = lax.bitcast_convert_type(words & _HI16, F32)
    return lo, hi


def _row_copy(src_hbm, row, dst_ref, r, sem):
    return pltpu.make_async_copy(src_hbm.at[pl.ds(row, 1)], dst_ref.at[pl.ds(r, 1)], sem)


def _gather_kernel(idx_ref, src_hbm, o_ref, sem, *, rows):
    def issue(pair, carry):
        for p in range(2):
            r = 2 * pair + p
            _row_copy(src_hbm, idx_ref[0, r], o_ref, r, sem).start(priority=p)
        return carry

    def drain(r, carry):
        _row_copy(src_hbm, 0, o_ref, r, sem).wait()
        return carry

    lax.fori_loop(0, rows // 2, issue, 0)
    lax.fori_loop(0, rows, drain, 0)


def _gather_rows(src, idx, rows):
    n_t = idx.shape[0] // rows
    w = src.shape[1]
    return pl.pallas_call(
        functools.partial(_gather_kernel, rows=rows),
        grid=(n_t,),
        in_specs=[pl.BlockSpec((None, 1, rows), lambda t: (t, 0, 0), memory_space=pltpu.SMEM),
                  pl.BlockSpec(memory_space=pl.ANY)],
        out_specs=pl.BlockSpec((rows, w), lambda t: (t, 0)),
        out_shape=jax.ShapeDtypeStruct((n_t * rows, w), src.dtype),
        scratch_shapes=[pltpu.SemaphoreType.DMA(())],
        compiler_params=_params(("arbitrary",), 4 * _nbytes((rows, w), src.dtype)),
        name="gather_rows",
    )(idx.reshape(n_t, 1, rows), src)


def _moe_up_kernel(te_ref, x_ref, wg_ref, wu_ref, o_ref, xb_ref):
    half = x_ref.shape[1]

    @pl.when(pl.program_id(1) == 0)
    def _():
        lo, hi = _unpack_halves(x_ref[...])
        xb_ref[:, :half] = lo.astype(BF16)
        xb_ref[:, half:] = hi.astype(BF16)

    x = xb_ref[...]
    g = jnp.dot(x, wg_ref[...], preferred_element_type=F32)
    u = jnp.dot(x, wu_ref[...], preferred_element_type=F32)
    o_ref[...] = (g * jax.nn.sigmoid(g) * u).astype(o_ref.dtype)


def _moe_up(xs, wg, wu, tile_expert):
    n_rows, half = xs.shape
    _, d, f = wg.shape
    rows = _MOE_ROWS
    tn = _divisor_tile(f, 512, _LANE)
    w_spec = pl.BlockSpec((None, d, tn), lambda t, j, te: (te[t], 0, j))
    vmem = 2 * (_nbytes((rows, half), jnp.uint32) + 2 * _nbytes((d, tn), BF16) + _nbytes((rows, tn), BF16)) \
        + _nbytes((rows, d), BF16) + 4 * _nbytes((rows, tn), F32)
    return pl.pallas_call(
        _moe_up_kernel,
        grid_spec=pltpu.PrefetchScalarGridSpec(
            num_scalar_prefetch=1,
            grid=(n_rows // rows, f // tn),
            in_specs=[pl.BlockSpec((rows, half), lambda t, j, te: (t, 0)), w_spec, w_spec],
            out_specs=pl.BlockSpec((rows, tn), lambda t, j, te: (t, j)),
            scratch_shapes=[pltpu.VMEM((rows, d), BF16)]),
        out_shape=jax.ShapeDtypeStruct((n_rows, f), BF16),
        compiler_params=_params(("arbitrary", "arbitrary"), vmem + (4 << 20)),
        name="moe_up",
    )(tile_expert, xs, wg, wu)


def _moe_down_kernel(te_ref, a_ref, wlo_ref, whi_ref, o_ref):
    a = a_ref[...]
    lo = jnp.dot(a, wlo_ref[...], preferred_element_type=F32)
    hi = jnp.dot(a, whi_ref[...], preferred_element_type=F32)
    o_ref[...] = _pack_halves(lo, hi)


def _moe_down(act, wd, tile_expert):
    n_rows, f = act.shape
    d = wd.shape[2]
    half = d // 2
    rows = _MOE_ROWS
    tn = _divisor_tile(half, 512, _LANE)
    nj = half // tn
    vmem = 2 * (_nbytes((rows, f), BF16) + 2 * _nbytes((f, tn), BF16) + _nbytes((rows, tn), jnp.uint32)) \
        + 4 * _nbytes((rows, tn), F32)
    return pl.pallas_call(
        _moe_down_kernel,
        grid_spec=pltpu.PrefetchScalarGridSpec(
            num_scalar_prefetch=1,
            grid=(n_rows // rows, nj),
            in_specs=[pl.BlockSpec((rows, f), lambda t, j, te: (t, 0)),
                      pl.BlockSpec((None, f, tn), lambda t, j, te: (te[t], 0, j)),
                      pl.BlockSpec((None, f, tn), lambda t, j, te: (te[t], 0, nj + j))],
            out_specs=pl.BlockSpec((rows, tn), lambda t, j, te: (t, j))),
        out_shape=jax.ShapeDtypeStruct((n_rows, half), jnp.uint32),
        compiler_params=_params(("arbitrary", "arbitrary"), vmem + (4 << 20)),
        name="moe_down",
    )(tile_expert, act, wd, wd)


def _moe_combine_kernel(i1_ref, i2_ref, y_hbm, r_ref, h_ref, g_ref, b_ref, hf_ref, hb_ref,
                        ya_ref, yb_ref, sem, *, rows, alpha):
    def issue(r, carry):
        _row_copy(y_hbm, i1_ref[0, r], ya_ref, r, sem).start(priority=0)
        _row_copy(y_hbm, i2_ref[0, r], yb_ref, r, sem).start(priority=1)
        return carry

    def drain(r, carry):
        _row_copy(y_hbm, 0, ya_ref, r, sem).wait()
        _row_copy(y_hbm, 0, yb_ref, r, sem).wait()
        return carry

    lax.fori_loop(0, rows, issue, 0)
    lax.fori_loop(0, rows, drain, 0)

    half = ya_ref.shape[1]
    w1 = r_ref[:, 0:1]
    w2 = r_ref[:, 1:2]
    a_lo, a_hi = _unpack_halves(ya_ref[...])
    b_lo, b_hi = _unpack_halves(yb_ref[...])
    z_lo = alpha * h_ref[:, :half] + (w1 * a_lo + w2 * b_lo)
    z_hi = alpha * h_ref[:, half:] + (w1 * a_hi + w2 * b_hi)
    inv_d = 1.0 / (2 * half)
    mu = (jnp.sum(z_lo, axis=-1, keepdims=True) + jnp.sum(z_hi, axis=-1, keepdims=True)) * inv_d
    c_lo = z_lo - mu
    c_hi = z_hi - mu
    var = (jnp.sum(c_lo * c_lo, axis=-1, keepdims=True) + jnp.sum(c_hi * c_hi, axis=-1, keepdims=True)) * inv_d
    rs = lax.rsqrt(var + LN_EPS)
    y_lo = c_lo * rs * g_ref[:, :half] + b_ref[:, :half]
    y_hi = c_hi * rs * g_ref[:, half:] + b_ref[:, half:]
    hf_ref[:, :half] = y_lo
    hf_ref[:, half:] = y_hi
    hb_ref[:, :half] = y_lo.astype(BF16)
    hb_ref[:, half:] = y_hi.astype(BF16)


def _moe_combine(ypk, dest1, dest2, top2, h, g, b, alpha):
    m, d = h.shape
    rows = _divisor_tile(m, _COMBINE_ROWS, _SUBLANE)
    n_t = m // rows
    idx_spec = pl.BlockSpec((None, 1, rows), lambda t: (t, 0, 0), memory_space=pltpu.SMEM)
    row = pl.BlockSpec((rows, d), lambda t: (t, 0))
    vec = pl.BlockSpec((1, d), lambda t: (0, 0))
    vmem = 2 * (_nbytes((rows, d), F32) * 2 + _nbytes((rows, d), BF16)) + 2 * _nbytes((rows, d // 2), F32) \
        + 6 * _nbytes((rows, d), F32)
    return pl.pallas_call(
        functools.partial(_moe_combine_kernel, rows=rows, alpha=alpha),
        grid=(n_t,),
        in_specs=[idx_spec, idx_spec, pl.BlockSpec(memory_space=pl.ANY),
                  pl.BlockSpec((rows, _LANE), lambda t: (t, 0)), row, vec, vec],
        out_specs=[row, row],
        out_shape=[jax.ShapeDtypeStruct((m, d), F32), jax.ShapeDtypeStruct((m, d), BF16)],
        scratch_shapes=[pltpu.VMEM((rows, d // 2), jnp.uint32), pltpu.VMEM((rows, d // 2), jnp.uint32),
                        pltpu.SemaphoreType.DMA(())],
        compiler_params=_params(("arbitrary",), vmem),
        name="moe_combine",
    )(dest1.reshape(n_t, 1, rows), dest2.reshape(n_t, 1, rows), ypk, top2, h,
      g.reshape(1, d).astype(F32), b.reshape(1, d).astype(F32))


def _moe_routing(top2, n_exp, rows):
    m = top2.shape[0]
    choice = top2[:, 2:4].astype(jnp.int32).reshape(-1)
    onehot = (choice[:, None] == jnp.arange(n_exp, dtype=jnp.int32)[None, :]).astype(jnp.int32)
    running = jnp.cumsum(onehot, axis=0)
    rank = jnp.sum(onehot * running, axis=1) - 1
    counts = running[-1]
    padded = ((counts + rows - 1) // rows) * rows
    ends = jnp.cumsum(padded)
    starts = ends - padded
    n_tiles = -(-2 * m // rows) + n_exp
    dest = jnp.clip(jnp.sum(onehot * starts[None, :], axis=1) + rank, 0, n_tiles * rows - 1)
    src_tok = jnp.zeros((n_tiles * rows,), jnp.int32).at[dest].set(
        jnp.arange(2 * m, dtype=jnp.int32) // 2)
    tile_start = jnp.arange(n_tiles, dtype=jnp.int32) * rows
    tile_expert = jnp.minimum(jnp.sum((tile_start[:, None] >= ends[None, :]).astype(jnp.int32), axis=1),
                              n_exp - 1)
    return src_tok, tile_expert, dest[0::2], dest[1::2]


def _attn_kernel(slopes_ref, q_ref, k_ref, v_ref, lq1_ref, lk1_ref, lq2_ref, lk2_ref, g_ref,
                 o_ref, qqt_ref, vt_ref, s_ref, m_ref, l_ref, acc_ref, *, tq, tk, qk_dim, scale, lam_init):
    h = pl.program_id(1)
    i = pl.program_id(2)
    slope = slopes_ref[h]
    r = tq // tk
    lp = k_ref.shape[0]
    vdim = v_ref.shape[1]

    @pl.when(i == 0)
    def _():
        def xpose(c, carry):
            c0 = pl.multiple_of(c * tk, tk)
            vt_ref[:, pl.ds(c0, tk)] = v_ref[pl.ds(c0, tk), :].astype(F32).T.astype(BF16)
            return carry
        lax.fori_loop(0, lp // tk, xpose, 0)

    log2e = math.log2(math.e)
    qs = q_ref[...].astype(F32) * (scale * log2e)
    lane = lax.broadcasted_iota(jnp.int32, qs.shape, 1)
    ones = jnp.where(lane < 3, 1.0, 0.0).astype(F32)
    qa1 = jnp.concatenate([jnp.where(lane < qk_dim, qs, 0.0), ones], axis=1)
    qa2 = jnp.concatenate([jnp.where(lane >= qk_dim, qs, 0.0), ones], axis=1)
    qqt_ref[:, :tq] = qa1.T.astype(BF16)
    qqt_ref[:, tq:] = qa2.T.astype(BF16)
    slope2 = slope * log2e
    kj_idx = lax.broadcasted_iota(jnp.int32, (tk, vdim), 0).astype(F32)
    kl_idx = lax.broadcasted_iota(jnp.int32, (tk, vdim), 1)
    bias = slope2 * kj_idx
    b_hi = bias.astype(BF16).astype(F32)
    rem = bias - b_hi
    b_mid = rem.astype(BF16).astype(F32)
    b_lo = rem - b_mid
    kb = jnp.where(kl_idx == 0, b_hi, jnp.where(kl_idx == 1, b_mid, jnp.where(kl_idx == 2, b_lo, 0.0)))
    kb = kb.astype(BF16)

    m_ref[...] = jnp.full(m_ref.shape, _NEG, F32)
    l_ref[...] = jnp.zeros(l_ref.shape, F32)
    acc_ref[...] = jnp.zeros(acc_ref.shape, F32)

    def put_scores(slot, j):
        k0 = pl.multiple_of(j * tk, tk)
        ka = jnp.concatenate([k_ref[pl.ds(k0, tk), :], kb], axis=1)
        for half in range(2):
            s_ref[slot, half] = jnp.dot(ka, qqt_ref[:, half * tq:(half + 1) * tq],
                                        preferred_element_type=F32)

    def absorb(j, slot, masked):
        k0 = pl.multiple_of(j * tk, tk)
        vt = vt_ref[:, pl.ds(k0, tk)]
        off = j * tk - i * tq
        c = slope2 * off.astype(F32)
        for half in range(2):
            s = s_ref[slot, half]
            if masked:
                krow = lax.broadcasted_iota(jnp.int32, s.shape, 0)
                qcol = lax.broadcasted_iota(jnp.int32, s.shape, 1)
                s = jnp.where(krow - qcol <= -off, s, _NEG)
            m_old = m_ref[half]
            m_new = jnp.maximum(m_old, jnp.max(s, axis=0, keepdims=True) + c)
            p = jnp.exp2(s + (c - m_new))
            alpha = jnp.exp2(m_old - m_new)
            l_ref[half] = alpha * l_ref[half] + jnp.sum(p, axis=0, keepdims=True)
            pv = jnp.dot(vt, p.astype(BF16), preferred_element_type=F32)
            acc_ref[half] = alpha * acc_ref[half] + pv
            m_ref[half] = m_new

    n_full = i * r
    odd = n_full & 1

    @pl.when(odd == 1)
    def _():
        put_scores(0, 0)
        absorb(0, 0, False)

    put_scores(0, odd)

    def pair(t, carry):
        j = odd + 2 * t
        put_scores(1, j + 1)
        absorb(j, 0, False)
        put_scores(0, j + 2)
        absorb(j + 1, 1, False)
        return carry

    lax.fori_loop(0, lax.shift_right_logical(n_full - odd, 1), pair, 0)
    for d in range(r):
        if d + 1 < r:
            put_scores((d + 1) % 2, n_full + d + 1)
        absorb(n_full + d, d % 2, True)

    lam = (jnp.exp(jnp.sum(lq1_ref[...] * lk1_ref[...], axis=1, keepdims=True))
           - jnp.exp(jnp.sum(lq2_ref[...] * lk2_ref[...], axis=1, keepdims=True)) + lam_init)
    ot = acc_ref[0] * (1.0 / l_ref[0]) - lam * (acc_ref[1] * (1.0 / l_ref[1]))
    ms = jnp.mean(ot * ot, axis=0, keepdims=True)
    y = ot * lax.rsqrt(ms + LN_EPS) * g_ref[...] * (1.0 - lam_init)
    o_ref[...] = y.T.astype(o_ref.dtype)


def _diff_attention(proj, bsz, lp, d_att, lam_vecs, subln_g, lam_init):
    vdim = d_att // ATT_HEADS
    qk_dim = vdim // 2
    assert vdim == _LANE
    tq = _divisor_tile(lp, _TS_TARGET, _SEQ_ALIGN)
    tk = _divisor_tile(tq, _TK_ATT, _LANE)
    nq = lp // tq
    slopes = jnp.asarray(2.0 ** (-8.0 * np.arange(1, ATT_HEADS + 1) / ATT_HEADS), F32)
    lq1, lk1, lq2, lk2 = [v.reshape(1, qk_dim).astype(F32) for v in lam_vecs]
    vec = pl.BlockSpec((1, qk_dim), lambda b, h, i: (0, 0))
    in_specs = [
        pl.BlockSpec(memory_space=pltpu.SMEM),
        pl.BlockSpec((tq, vdim), lambda b, h, i: (b * nq + i, h)),
        pl.BlockSpec((lp, vdim), lambda b, h, i: (b, ATT_HEADS + h)),
        pl.BlockSpec((lp, vdim), lambda b, h, i: (b, 2 * ATT_HEADS + h)),
        vec, vec, vec, vec,
        pl.BlockSpec((vdim, 1), lambda b, h, i: (0, 0)),
    ]
    vmem = 2 * (2 * _nbytes((lp, vdim), BF16) + 2 * _nbytes((tq, vdim), BF16)) \
        + _nbytes((lp, vdim), BF16) + 4 * _nbytes((2 * tq, vdim), F32) + 6 * _nbytes((2 * tq, tk), F32)
    return pl.pallas_call(
        functools.partial(_attn_kernel, tq=tq, tk=tk, qk_dim=qk_dim, scale=qk_dim ** -0.5,
                          lam_init=lam_init),
        grid=(bsz, ATT_HEADS, nq),
        in_specs=in_specs,
        out_specs=pl.BlockSpec((tq, vdim), lambda b, h, i: (b * nq + i, h)),
        out_shape=jax.ShapeDtypeStruct((bsz * lp, d_att), BF16),
        scratch_shapes=[pltpu.VMEM((2 * vdim, 2 * tq), BF16), pltpu.VMEM((vdim, lp), BF16),
                        pltpu.VMEM((2, 2, tk, tq), F32),
                        pltpu.VMEM((2, 1, tq), F32), pltpu.VMEM((2, 1, tq), F32),
                        pltpu.VMEM((2, vdim, tq), F32)],
        compiler_params=_params(("parallel", "parallel", "arbitrary"), vmem + (8 << 20)),
        name="diff_attention",
    )(slopes, proj, proj, proj, lq1, lk1, lq2, lk2, subln_g.reshape(vdim, 1).astype(F32))


_CONV_HALO = 32
_CONV_ROWS = 32


def _conv_kernel(a_ref, gate_ref, dw_ref, dwb_ref, lng_ref, lnb_ref, pw_ref, pwb_ref, o_ref,
                 hbuf_ref, cbuf_ref, sh_ref, *, ts, k_len):
    t = pl.program_id(1)
    c_dim = a_ref.shape[1]

    @pl.when(t == 0)
    def _():
        hbuf_ref[0:_CONV_HALO, :] = jnp.zeros((_CONV_HALO, c_dim), F32)

    @pl.when(t > 0)
    def _():
        hbuf_ref[0:_CONV_HALO, :] = hbuf_ref[ts:ts + _CONV_HALO, :]

    a = a_ref[...].astype(F32)
    hbuf_ref[_CONV_HALO:_CONV_HALO + ts, :] = a * jax.nn.sigmoid(gate_ref[...].astype(F32))

    first = _CONV_HALO - (k_len - 1)
    sh_rows = sh_ref.shape[1]
    for lb in range(c_dim // _LANE):
        lanes = slice(lb * _LANE, (lb + 1) * _LANE)
        for b in range(1, _SUBLANE):
            sh_ref[b - 1] = hbuf_ref[b:b + sh_rows, lanes]
        taps = [dw_ref[k:k + 1, lanes] for k in range(k_len)]
        bias = dwb_ref[:, lanes]

        def chunk(c, carry, lanes=lanes, taps=taps, bias=bias):
            base = c * _CONV_ROWS
            acc = jnp.broadcast_to(bias, (_CONV_ROWS, _LANE))
            for k in range(k_len):
                a, b = divmod(first + k, _SUBLANE)
                start = pl.multiple_of(base + a * _SUBLANE, _SUBLANE)
                if b == 0:
                    src = hbuf_ref[pl.ds(start, _CONV_ROWS), lanes]
                else:
                    src = sh_ref[b - 1, pl.ds(start, _CONV_ROWS), :]
                acc = acc + taps[k] * src
            cbuf_ref[pl.ds(pl.multiple_of(base, _CONV_ROWS), _CONV_ROWS), lanes] = acc
            return carry

        lax.fori_loop(0, ts // _CONV_ROWS, chunk, 0)

    y = cbuf_ref[...]
    mu = jnp.mean(y, axis=-1, keepdims=True)
    yc = y - mu
    var = jnp.mean(yc * yc, axis=-1, keepdims=True)
    yn = yc * lax.rsqrt(var + LN_EPS) * lng_ref[...] + lnb_ref[...]
    act = (yn * jax.nn.sigmoid(yn)).astype(BF16)
    out = jnp.dot(act, pw_ref[...], preferred_element_type=F32) + pwb_ref[...]
    o_ref[...] = out.astype(o_ref.dtype)


def _conformer_conv(proj, bsz, lp, col0, d_conv, dw_w, dw_b, ln_g, ln_b, pw_w, pw_b):
    ts = _divisor_tile(lp, _TS_TARGET, _CONV_ROWS)
    nt = lp // ts
    cb = col0 // d_conv
    assert col0 % d_conv == 0 and dw_w.shape[0] - 1 <= _CONV_HALO
    vec = pl.BlockSpec((1, d_conv), lambda b, t: (0, 0))
    in_specs = [
        pl.BlockSpec((ts, d_conv), lambda b, t: (b * nt + t, cb)),
        pl.BlockSpec((ts, d_conv), lambda b, t: (b * nt + t, cb + 1)),
        pl.BlockSpec((dw_w.shape[0], d_conv), lambda b, t: (0, 0)),
        vec, vec, vec,
        pl.BlockSpec((d_conv, d_conv), lambda b, t: (0, 0)),
        vec,
    ]
    vmem = 4 * _nbytes((ts, d_conv), BF16) * 2 + 8 * _nbytes((ts + _CONV_HALO, d_conv), F32)
    r1 = lambda v: v.reshape(1, d_conv).astype(F32)
    return pl.pallas_call(
        functools.partial(_conv_kernel, ts=ts, k_len=dw_w.shape[0]),
        grid=(bsz, nt),
        in_specs=in_specs,
        out_specs=pl.BlockSpec((ts, d_conv), lambda b, t: (b * nt + t, 0)),
        out_shape=jax.ShapeDtypeStruct((bsz * lp, d_conv), BF16),
        scratch_shapes=[pltpu.VMEM((ts + _CONV_HALO, d_conv), F32), pltpu.VMEM((ts, d_conv), F32),
                        pltpu.VMEM((_SUBLANE - 1, ts + _CONV_HALO - _SUBLANE, _LANE), F32)],
        compiler_params=_params(("parallel", "arbitrary"), vmem),
        name="conformer_conv",
    )(proj, proj, dw_w.astype(F32), r1(dw_b), r1(ln_g), r1(ln_b), pw_w.astype(BF16), r1(pw_b))


_POOL_HALO = 16


def _pool_kernel(u_ref, w_ref, b_ref, s_ref, o_ref, buf_ref, *, ts, windows, gw):
    t = pl.program_id(1)
    c_dim = u_ref.shape[1]

    @pl.when(t == 0)
    def _():
        buf_ref[0:_POOL_HALO, :] = jnp.zeros((_POOL_HALO, c_dim), F32)

    @pl.when(t > 0)
    def _():
        buf_ref[0:_POOL_HALO, :] = buf_ref[ts:ts + _POOL_HALO, :]

    buf_ref[_POOL_HALO:_POOL_HALO + ts, :] = u_ref[...].astype(F32)
    pos = (t * ts + 1 + lax.broadcasted_iota(jnp.int32, (ts, 1), 0)).astype(F32)
    for gi, win in enumerate(windows):
        lanes = slice(gi * gw, (gi + 1) * gw)
        u = buf_ref[_POOL_HALO:_POOL_HALO + ts, lanes]
        tot = u
        for j in range(1, win):
            tot = tot + buf_ref[_POOL_HALO - j:_POOL_HALO - j + ts, lanes]
        d = tot / jnp.minimum(pos, float(win)) - u
        y = jnp.dot(d.astype(BF16), w_ref[gi], preferred_element_type=F32)
        o_ref[:, lanes] = ((y + b_ref[:, lanes]) * s_ref[:, lanes]).astype(o_ref.dtype)


def _multiscale_pool(proj, bsz, lp, col0, d_pool, w, bias, scale):
    n_g = len(POOL_WINDOWS)
    gw = d_pool // n_g
    ts = _divisor_tile(lp, _TS_TARGET, _SUBLANE)
    nt = lp // ts
    assert col0 % d_pool == 0 and max(POOL_WINDOWS) - 1 <= _POOL_HALO
    cb = col0 // d_pool
    vec = pl.BlockSpec((1, d_pool), lambda b, t: (0, 0))
    r1 = lambda v: v.reshape(1, d_pool).astype(F32)
    return pl.pallas_call(
        functools.partial(_pool_kernel, ts=ts, windows=POOL_WINDOWS, gw=gw),
        grid=(bsz, nt),
        in_specs=[pl.BlockSpec((ts, d_pool), lambda b, t: (b * nt + t, cb)),
                  pl.BlockSpec((n_g, gw, gw), lambda b, t: (0, 0, 0)), vec, vec],
        out_specs=pl.BlockSpec((ts, d_pool), lambda b, t: (b * nt + t, 0)),
        out_shape=jax.ShapeDtypeStruct((bsz * lp, d_pool), BF16),
        scratch_shapes=[pltpu.VMEM((ts + _POOL_HALO, d_pool), F32)],
        compiler_params=_params(("parallel", "arbitrary"), 10 * _nbytes((ts, d_pool), F32)),
        name="multiscale_pool",
    )(proj, w.astype(BF16), r1(bias), r1(scale))


_SSD_HALO = 8


def _ssd_kernel(z_ref, x_ref, bc_ref, dt_ref, cw_ref, cb_ref, dtb_ref, alog_ref, dskip_ref, ng_ref,
                expand_ref, o_ref, ext_ref, state_ref, *, tc, d_ssm, k_len):
    c = pl.program_id(1)
    n_state = SSM_STATE
    gwid = d_ssm // SSM_GROUPS
    hpg = gwid // SSM_HEADDIM
    xbc_dim = x_ref.shape[1] + bc_ref.shape[1]

    @pl.when(c == 0)
    def _():
        ext_ref[0:_SSD_HALO, :] = jnp.zeros((_SSD_HALO, xbc_dim), F32)
        state_ref[...] = jnp.zeros(state_ref.shape, F32)

    @pl.when(c > 0)
    def _():
        ext_ref[0:_SSD_HALO, :] = ext_ref[tc:tc + _SSD_HALO, :]

    ext_ref[_SSD_HALO:_SSD_HALO + tc, :d_ssm] = x_ref[...].astype(F32)
    ext_ref[_SSD_HALO:_SSD_HALO + tc, d_ssm:] = bc_ref[...].astype(F32)

    first = _SSD_HALO - (k_len - 1)
    pre = jnp.broadcast_to(cb_ref[...], (tc, xbc_dim))
    for k in range(k_len):
        pre = pre + cw_ref[k:k + 1, :] * ext_ref[first + k:first + k + tc, :]
    xbc = pre * jax.nn.sigmoid(pre)
    xs = xbc[:, :d_ssm]
    bm = xbc[:, d_ssm:d_ssm + SSM_GROUPS * n_state].astype(BF16)
    cm = xbc[:, d_ssm + SSM_GROUPS * n_state:].astype(BF16)

    dt = jax.nn.softplus(dt_ref[...] + dtb_ref[...])
    a = dt * (-jnp.exp(alog_ref[...]))
    ti = lax.broadcasted_iota(jnp.int32, (tc, tc), 0)
    si = lax.broadcasted_iota(jnp.int32, (tc, tc), 1)
    causal = ti >= si
    tri = jnp.where(causal, 1.0, 0.0).astype(F32)
    a_cs = jnp.dot(tri, a, preferred_element_type=F32, precision=HIGHEST)
    a_cs_t = a_cs.T
    expand = expand_ref[...]
    a_cs_x = jnp.dot(a_cs, expand, preferred_element_type=F32, precision=HIGHEST)
    dt_x = jnp.dot(dt, expand, preferred_element_type=F32, precision=HIGHEST)
    a_last = a_cs_x[tc - 1:tc, :]
    x_dt = xs * dt_x
    x_end = (x_dt * jnp.exp(a_last - a_cs_x)).astype(BF16)
    x_dt_b = x_dt.astype(BF16)
    decay_in = jnp.exp(a_cs_x)
    decay_chunk = jnp.exp(a_last)

    lane_g = lax.broadcasted_iota(jnp.int32, (tc, gwid), 1)
    ys = []
    for g in range(SSM_GROUPS):
        cg = cm[:, g * n_state:(g + 1) * n_state]
        bg = bm[:, g * n_state:(g + 1) * n_state]
        gl = slice(g * gwid, (g + 1) * gwid)
        cb = lax.dot_general(cg, bg, (((1,), (1,)), ((), ())), preferred_element_type=F32)
        xg = x_dt_b[:, gl]
        y_g = None
        for r in range(hpg):
            hd = g * hpg + r
            seg = a_cs[:, hd:hd + 1] - a_cs_t[hd:hd + 1, :]
            lmat = jnp.exp(jnp.where(causal, seg, _NEG))
            mh = (cb * lmat).astype(BF16)
            in_head = (lane_g >= r * SSM_HEADDIM) & (lane_g < (r + 1) * SSM_HEADDIM)
            xh = jnp.where(in_head, xg, jnp.zeros_like(xg))
            yh = jnp.dot(mh, xh, preferred_element_type=F32)
            y_g = yh if y_g is None else y_g + yh
        st = state_ref[g]
        y_g = y_g + jnp.dot(cg, st.astype(BF16), preferred_element_type=F32) * decay_in[:, gl]
        upd = lax.dot_general(bg, x_end[:, gl], (((0,), (0,)), ((), ())), preferred_element_type=F32)
        state_ref[g] = decay_chunk[:, gl] * st + upd
        ys.append(y_g)
    y = jnp.concatenate(ys, axis=1) + dskip_ref[...] * xs
    zf = z_ref[...].astype(F32)
    y = y * (zf * jax.nn.sigmoid(zf))
    outs = []
    for g in range(SSM_GROUPS):
        yg = y[:, g * gwid:(g + 1) * gwid]
        ms = jnp.mean(yg * yg, axis=-1, keepdims=True)
        outs.append(yg * lax.rsqrt(ms + LN_EPS))
    o_ref[...] = (jnp.concatenate(outs, axis=1) * ng_ref[...]).astype(o_ref.dtype)


def _mamba2_ssd(proj, dt_raw, bsz, lp, z_col0, d_ssm, conv_w, conv_b, dt_bias, a_log, d_skip, norm_g):
    xbc_dim = d_ssm + 2 * SSM_GROUPS * SSM_STATE
    n_heads = d_ssm // SSM_HEADDIM
    tc = _SSD_CHUNK
    nc = lp // tc
    bc_dim = xbc_dim - d_ssm
    assert lp % tc == 0 and z_col0 % d_ssm == 0 and (z_col0 + 2 * d_ssm) % bc_dim == 0
    zb = z_col0 // d_ssm
    bcb = (z_col0 + 2 * d_ssm) // bc_dim
    pad_h = lambda v: jnp.pad(v.astype(F32).reshape(1, n_heads), ((0, 0), (0, _LANE - n_heads)))
    expand = (np.arange(d_ssm)[None, :] // SSM_HEADDIM == np.arange(_LANE)[:, None]).astype(np.float32)
    d_x = jnp.repeat(d_skip.astype(F32), SSM_HEADDIM).reshape(1, d_ssm)
    full = lambda shape: pl.BlockSpec(shape, lambda b, c: (0,) * len(shape))
    in_specs = [
        pl.BlockSpec((tc, d_ssm), lambda b, c: (b * nc + c, zb)),
        pl.BlockSpec((tc, d_ssm), lambda b, c: (b * nc + c, zb + 1)),
        pl.BlockSpec((tc, bc_dim), lambda b, c: (b * nc + c, bcb)),
        pl.BlockSpec((tc, _LANE), lambda b, c: (b * nc + c, 0)),
        full((conv_w.shape[0], xbc_dim)), full((1, xbc_dim)),
        full((1, _LANE)), full((1, _LANE)), full((1, d_ssm)), full((1, d_ssm)),
        full((_LANE, d_ssm)),
    ]
    return pl.pallas_call(
        functools.partial(_ssd_kernel, tc=tc, d_ssm=d_ssm, k_len=conv_w.shape[0]),
        grid=(bsz, nc),
        in_specs=in_specs,
        out_specs=pl.BlockSpec((tc, d_ssm), lambda b, c: (b * nc + c, 0)),
        out_shape=jax.ShapeDtypeStruct((bsz * lp, d_ssm), BF16),
        scratch_shapes=[pltpu.VMEM((tc + _SSD_HALO, xbc_dim), F32),
                        pltpu.VMEM((SSM_GROUPS, SSM_STATE, d_ssm // SSM_GROUPS), F32)],
        compiler_params=_params(("parallel", "arbitrary"), 32 << 20),
        name="mamba2_ssd",
    )(proj, proj, proj, dt_raw, conv_w.astype(F32), conv_b.reshape(1, xbc_dim).astype(F32),
      pad_h(dt_bias), pad_h(a_log), d_x, norm_g.reshape(1, d_ssm).astype(F32), jnp.asarray(expand))


def kernel(x, meta, emb_ln_g, emb_ln_b, w_in, lam_q1, lam_k1, lam_q2, lam_k2, att_subln_g, cf_dw_w, cf_dw_b, cf_ln_g, cf_ln_b, cf_pw_w, cf_pw_b, pool_w, pool_b, pool_scale, ssm_conv_w, ssm_conv_b, ssm_dt_bias, ssm_a_log, ssm_d, ssm_norm_g, w_out, ln1_g, ln1_b, ln2_g, ln2_b, ffn_w_gate, ffn_w_up, ffn_w_down, moe_router, moe_w_gate, moe_w_up, moe_w_down):
    bsz, seq, d_model = x.shape
    depth = w_in.shape[0]
    n_meta = meta.shape[0]
    d_att = d_conv = d_pool = d_ssm = d_model // 4
    xbc_dim = d_ssm + 2 * SSM_GROUPS * SSM_STATE
    n_heads = d_ssm // SSM_HEADDIM
    d_main = 3 * d_att + 2 * d_conv + d_pool + d_ssm + xbc_dim
    alpha = (2 * depth) ** 0.25

    seq_len = n_meta + seq
    lp = -(-seq_len // _SEQ_ALIGN) * _SEQ_ALIGN
    m = bsz * lp
    meta_b = jnp.broadcast_to(meta[None].astype(x.dtype), (bsz, n_meta, d_model))
    tokens = jnp.concatenate([meta_b, x, jnp.zeros((bsz, lp - seq_len, d_model), x.dtype)], axis=1)
    h, h16 = _layer_norm(tokens.reshape(m, d_model), emb_ln_g, emb_ln_b)

    for l in range(depth):
        proj = _matmul([h16], _weight_bf16(w_in, l, 1, d_main)[0], BF16)
        w_dt = jnp.pad(w_in[l, :, d_main:], ((0, 0), (0, _LANE - n_heads))).astype(BF16)
        dt_raw = _matmul([h16], w_dt, F32)
        lam_init = 0.8 - 0.6 * math.exp(-0.3 * l)

        y_att = _diff_attention(proj, bsz, lp, d_att, (lam_q1[l], lam_k1[l], lam_q2[l], lam_k2[l]),
                                att_subln_g[l], lam_init)
        y_conv = _conformer_conv(proj, bsz, lp, 3 * d_att, d_conv, cf_dw_w[l], cf_dw_b[l],
                                 cf_ln_g[l], cf_ln_b[l], cf_pw_w[l], cf_pw_b[l])
        y_pool = _multiscale_pool(proj, bsz, lp, 3 * d_att + 2 * d_conv, d_pool,
                                  pool_w[l], pool_b[l], pool_scale[l])
        y_ssm = _mamba2_ssd(proj, dt_raw, bsz, lp, 3 * d_att + 2 * d_conv + d_pool, d_ssm,
                            ssm_conv_w[l], ssm_conv_b[l], ssm_dt_bias[l], ssm_a_log[l], ssm_d[l],
                            ssm_norm_g[l])
        z = _matmul([y_att, y_conv, y_pool, y_ssm], _weight_bf16(w_out, l, 1)[0], F32,
                    residual=h, alpha=alpha)

        e = l // 2
        if l % 2 == 0:
            h, h16 = _layer_norm(z, ln1_g[l], ln1_b[l])
            act = _swiglu_up(h16, _weight_bf16(ffn_w_gate, e, 1)[0], _weight_bf16(ffn_w_up, e, 1)[0])
            z = _matmul([act], _weight_bf16(ffn_w_down, e, 1)[0], F32, residual=h, alpha=alpha)
            h, h16 = _layer_norm(z, ln2_g[l], ln2_b[l])
        else:
            n_exp = moe_router.shape[2]
            expert_bf16 = lambda w: _weight_bf16(w.reshape((-1,) + w.shape[2:]), e * n_exp, n_exp)
            h, h16, h_packed = _layer_norm(z, ln1_g[l], ln1_b[l], packed=True)
            top2 = _router_top2(h, moe_router[e])
            src_tok, tile_expert, dest1, dest2 = _moe_routing(top2, n_exp, _MOE_ROWS)
            xs = _gather_rows(h_packed, src_tok, _MOE_ROWS)
            act = _moe_up(xs, expert_bf16(moe_w_gate), expert_bf16(moe_w_up), tile_expert)
            y_sorted = _moe_down(act, expert_bf16(moe_w_down), tile_expert)
            h, h16 = _moe_combine(y_sorted, dest1, dest2, top2, h, ln2_g[l], ln2_b[l], alpha)

    return h.reshape(bsz, lp, d_model)[:, n_meta:seq_len]
```

```python
import functools
import math

import numpy as np
import jax
import jax.numpy as jnp
from jax import lax
from jax.experimental import pallas as pl
from jax.experimental.pallas import tpu as pltpu

F32 = jnp.float32
BF16 = jnp.bfloat16
HIGHEST = lax.Precision.HIGHEST

N_META = 16
ATT_HEADS = 8
CONV_K = 31
POOL_WINDOWS = (2, 4, 8, 16)
SSM_HEADDIM = 64
SSM_GROUPS = 4
SSM_STATE = 128
SSM_CONV_K = 4
N_EXPERTS = 8
LN_EPS = 1e-5

_LANE = 128
_SUBLANE = 8
_VMEM_CAP = 60 * 1024 * 1024

_SEQ_ALIGN = 256
_TM_TARGET = 1536
_TS_TARGET = 768
_TK_ATT = 256
_SSD_CHUNK = 128
_LN_ROWS = 256
_NEG = -1e30


def _divisor_tile(n, target, align):
    best = align
    t = align
    while t <= min(n, target):
        if n % t == 0:
            best = t
        t += align
    assert n % best == 0, (n, target, align)
    return best


def _params(sem, vmem_bytes):
    return pltpu.CompilerParams(
        dimension_semantics=sem,
        vmem_limit_bytes=int(min(max(vmem_bytes, 32 * 1024 * 1024), _VMEM_CAP)))


def _nbytes(shape, dtype):
    return int(np.prod(shape)) * jnp.dtype(dtype).itemsize


def _ln_kernel(z_ref, g_ref, b_ref, hf_ref, hb_ref, *maybe_packed_ref):
    z = z_ref[...]
    mu = jnp.mean(z, axis=-1, keepdims=True)
    zc = z - mu
    var = jnp.mean(zc * zc, axis=-1, keepdims=True)
    y = zc * lax.rsqrt(var + LN_EPS) * g_ref[...] + b_ref[...]
    hf_ref[...] = y
    hb_ref[...] = y.astype(BF16)
    if maybe_packed_ref:
        half = y.shape[1] // 2
        maybe_packed_ref[0][...] = _pack_halves(y[:, :half], y[:, half:])


def _layer_norm(z, g, b, packed=False):
    m, d = z.shape
    tm = _divisor_tile(m, _LN_ROWS, _SUBLANE)
    row = pl.BlockSpec((tm, d), lambda i: (i, 0))
    vec = pl.BlockSpec((1, d), lambda i: (0, 0))
    out_specs = [row, row]
    out_shape = [jax.ShapeDtypeStruct((m, d), F32), jax.ShapeDtypeStruct((m, d), BF16)]
    if packed:
        out_specs.append(pl.BlockSpec((tm, d // 2), lambda i: (i, 0)))
        out_shape.append(jax.ShapeDtypeStruct((m, d // 2), jnp.uint32))
    return pl.pallas_call(
        _ln_kernel,
        grid=(m // tm,),
        in_specs=[row, vec, vec],
        out_specs=out_specs,
        out_shape=out_shape,
        compiler_params=_params(("parallel",), 10 * _nbytes((tm, d), F32)),
        name="layer_norm",
    )(z, g.reshape(1, d).astype(F32), b.reshape(1, d).astype(F32))


def _embed_ln_kernel(prev_ref, cur_ref, meta_ref, g_ref, b_ref, hf_ref, hb_ref, *, rows, seq_len):
    t = pl.program_id(1)
    n_meta = meta_ref.shape[0]
    top = jnp.where(t == 0, meta_ref[...], prev_ref[rows - n_meta:, :])
    tok = jnp.concatenate([top, cur_ref[:rows - n_meta, :]], axis=0)
    pos = t * rows + lax.broadcasted_iota(jnp.int32, (rows, 1), 0)
    z = jnp.where(pos < seq_len, tok, 0.0)
    mu = jnp.mean(z, axis=-1, keepdims=True)
    zc = z - mu
    var = jnp.mean(zc * zc, axis=-1, keepdims=True)
    y = zc * lax.rsqrt(var + LN_EPS) * g_ref[...] + b_ref[...]
    hf_ref[...] = y
    hb_ref[...] = y.astype(BF16)


def _embed_layer_norm(x, meta, g, b, lp):
    bsz, seq, d = x.shape
    n_meta = meta.shape[0]
    rows = _divisor_tile(math.gcd(seq, lp), _LN_ROWS, _SUBLANE)
    assert n_meta % _SUBLANE == 0 and n_meta <= rows
    nx = seq // rows
    nt = lp // rows
    out_row = pl.BlockSpec((rows, d), lambda bi, t: (bi * nt + t, 0))
    vec = pl.BlockSpec((1, d), lambda bi, t: (0, 0))
    return pl.pallas_call(
        functools.partial(_embed_ln_kernel, rows=rows, seq_len=n_meta + seq),
        grid=(bsz, nt),
        in_specs=[pl.BlockSpec((None, rows, d), lambda bi, t: (bi, jnp.clip(t - 1, 0, nx - 1), 0)),
                  pl.BlockSpec((None, rows, d), lambda bi, t: (bi, jnp.minimum(t, nx - 1), 0)),
                  pl.BlockSpec((n_meta, d), lambda bi, t: (0, 0)), vec, vec],
        out_specs=[out_row, out_row],
        out_shape=[jax.ShapeDtypeStruct((bsz * lp, d), F32), jax.ShapeDtypeStruct((bsz * lp, d), BF16)],
        compiler_params=_params(("parallel", "parallel"), 12 * _nbytes((rows, d), F32)),
        name="embed_layer_norm",
    )(x, x, meta.astype(F32), g.reshape(1, d).astype(F32), b.reshape(1, d).astype(F32))


def _cast_kernel(w_ref, o_ref):
    o_ref[...] = w_ref[...].astype(o_ref.dtype)


def _weight_bf16(w, s0, n_s, n_cols=None):
    _, r, c = w.shape
    n_cols = c if n_cols is None else n_cols
    rb = _divisor_tile(r, 512, _SUBLANE)
    cb = _divisor_tile(n_cols, 2048, _LANE)
    return pl.pallas_call(
        _cast_kernel,
        grid=(n_s, r // rb, n_cols // cb),
        in_specs=[pl.BlockSpec((None, rb, cb), lambda s, i, j: (s0 + s, i, j))],
        out_specs=pl.BlockSpec((None, rb, cb), lambda s, i, j: (s, i, j)),
        out_shape=jax.ShapeDtypeStruct((n_s, r, n_cols), BF16),
        compiler_params=_params(("parallel", "parallel", "parallel"), 4 * _nbytes((rb, cb), F32)),
        name="weight_bf16",
    )(w)


def _mm_kernel(*refs, n_in, alpha, has_res):
    x_refs = refs[:n_in]
    w_refs = refs[n_in:2 * n_in]
    o_ref = refs[-1]
    acc = None
    for x_ref, w_ref in zip(x_refs, w_refs):
        d = jnp.dot(x_ref[...], w_ref[...], preferred_element_type=F32)
        acc = d if acc is None else acc + d
    if has_res:
        acc = acc + alpha * refs[2 * n_in][...]
    o_ref[...] = acc.astype(o_ref.dtype)


def _matmul(xs, w, out_dtype, residual=None, alpha=1.0):
    n_in = len(xs)
    m, kp = xs[0].shape
    k_total, n = w.shape
    assert k_total == n_in * kp
    tm = _divisor_tile(m, _TM_TARGET if k_total <= 4096 else _TM_TARGET // 2, _LANE)
    tn = _divisor_tile(n, 512 if k_total <= 4096 else 256, _LANE)
    if k_total > 8192:
        tm = _divisor_tile(m, _TM_TARGET // 4, _LANE)
    in_specs = [pl.BlockSpec((tm, kp), lambda i, j: (i, 0)) for _ in range(n_in)]
    in_specs += [pl.BlockSpec((kp, tn), lambda i, j, p=p: (p, j)) for p in range(n_in)]
    args = list(xs) + [w] * n_in
    vmem = 2 * (n_in * _nbytes((tm, kp), BF16) + n_in * _nbytes((kp, tn), BF16)
                + _nbytes((tm, tn), out_dtype)) + 2 * _nbytes((tm, tn), F32)
    if residual is not None:
        in_specs.append(pl.BlockSpec((tm, tn), lambda i, j: (i, j)))
        args.append(residual)
        vmem += 2 * _nbytes((tm, tn), F32)
    return pl.pallas_call(
        functools.partial(_mm_kernel, n_in=n_in, alpha=alpha, has_res=residual is not None),
        grid=(m // tm, n // tn),
        in_specs=in_specs,
        out_specs=pl.BlockSpec((tm, tn), lambda i, j: (i, j)),
        out_shape=jax.ShapeDtypeStruct((m, n), out_dtype),
        compiler_params=_params(("parallel", "arbitrary"), vmem + (4 << 20)),
        name="matmul",
    )(*args)


def _glu_kernel(x_ref, wg_ref, wu_ref, o_ref):
    x = x_ref[...]
    g = jnp.dot(x, wg_ref[...], preferred_element_type=F32)
    u = jnp.dot(x, wu_ref[...], preferred_element_type=F32)
    o_ref[...] = (g * jax.nn.sigmoid(g) * u).astype(o_ref.dtype)


def _swiglu_up(x, wg, wu):
    m, d = x.shape
    f = wg.shape[1]
    tm = _divisor_tile(m, _TM_TARGET, _LANE)
    tn = _divisor_tile(f, 256, _LANE)
    w_spec = pl.BlockSpec((d, tn), lambda i, j: (0, j))
    vmem = 2 * (_nbytes((tm, d), BF16) + 2 * _nbytes((d, tn), BF16) + _nbytes((tm, tn), BF16)) \
        + 4 * _nbytes((tm, tn), F32)
    return pl.pallas_call(
        _glu_kernel,
        grid=(m // tm, f // tn),
        in_specs=[pl.BlockSpec((tm, d), lambda i, j: (i, 0)), w_spec, w_spec],
        out_specs=pl.BlockSpec((tm, tn), lambda i, j: (i, j)),
        out_shape=jax.ShapeDtypeStruct((m, f), BF16),
        compiler_params=_params(("parallel", "arbitrary"), vmem + (4 << 20)),
        name="swiglu_up",
    )(x, wg, wu)


def _router_kernel(h_ref, r_ref, o_ref, *, n_exp):
    logits = jnp.dot(h_ref[...], r_ref[...], preferred_element_type=F32, precision=HIGHEST)
    lane = lax.broadcasted_iota(jnp.int32, logits.shape, 1).astype(F32)
    lg = jnp.where(lane < n_exp, logits, _NEG)
    m1 = jnp.max(lg, axis=1, keepdims=True)
    i1 = jnp.min(jnp.where(lg == m1, lane, float(_LANE)), axis=1, keepdims=True)
    lg2 = jnp.where(lane == i1, _NEG, lg)
    m2 = jnp.max(lg2, axis=1, keepdims=True)
    i2 = jnp.min(jnp.where(lg2 == m2, lane, float(_LANE)), axis=1, keepdims=True)
    e = jnp.exp(m2 - m1)
    w1 = 1.0 / (1.0 + e)
    w2 = e / (1.0 + e)
    o_ref[...] = (jnp.where(lane == 0.0, w1, 0.0) + jnp.where(lane == 1.0, w2, 0.0)
                  + jnp.where(lane == 2.0, i1, 0.0) + jnp.where(lane == 3.0, i2, 0.0))


def _router_top2(h, router):
    m, d = h.shape
    n_exp = router.shape[1]
    tm = _divisor_tile(m, 512, _SUBLANE)
    r_pad = jnp.pad(router.astype(F32), ((0, 0), (0, _LANE - n_exp)))
    return pl.pallas_call(
        functools.partial(_router_kernel, n_exp=n_exp),
        grid=(m // tm,),
        in_specs=[pl.BlockSpec((tm, d), lambda i: (i, 0)), pl.BlockSpec((d, _LANE), lambda i: (0, 0))],
        out_specs=pl.BlockSpec((tm, _LANE), lambda i: (i, 0)),
        out_shape=jax.ShapeDtypeStruct((m, _LANE), F32),
        compiler_params=_params(("parallel",), 4 * _nbytes((tm, d), F32) + (8 << 20)),
        name="router",
    )(h, r_pad)


_HI16 = np.uint32(0xFFFF0000)
_MOE_ROWS = 512
_COMBINE_ROWS = 256


def _pack_halves(lo, hi):
    lo_bits = lax.bitcast_convert_type(lo.astype(BF16).astype(F32), jnp.uint32)
    hi_bits = lax.bitcast_convert_type(hi.astype(BF16).astype(F32), jnp.uint32)
    return (lo_bits >> 16) | (hi_bits & _HI16)


def _unpack_halves(words):
    lo = lax.bitcast_convert_type(words << 16, F32)
    hi = lax.bitcast_convert_type(words & _HI16, F32)
    return lo, hi


def _row_copy(src_hbm, row, dst_ref, r, sem):
    return pltpu.make_async_copy(src_hbm.at[pl.ds(row, 1)], dst_ref.at[pl.ds(r, 1)], sem)


def _wait_rows(src_hbm, dst_ref, sem):
    pltpu.make_async_copy(src_hbm.at[pl.ds(0, dst_ref.shape[0])], dst_ref, sem).wait()


def _gather_kernel(idx_ref, src_hbm, o_ref, sem, *, rows):
    def issue(pair, carry):
        for p in range(2):
            r = 2 * pair + p
            _row_copy(src_hbm, idx_ref[0, r], o_ref, r, sem).start(priority=p)
        return carry

    lax.fori_loop(0, rows // 2, issue, 0)
    _wait_rows(src_hbm, o_ref, sem)


def _gather_rows(src, idx, rows):
    n_t = idx.shape[0] // rows
    w = src.shape[1]
    return pl.pallas_call(
        functools.partial(_gather_kernel, rows=rows),
        grid=(n_t,),
        in_specs=[pl.BlockSpec((None, 1, rows), lambda t: (t, 0, 0), memory_space=pltpu.SMEM),
                  pl.BlockSpec(memory_space=pl.ANY)],
        out_specs=pl.BlockSpec((rows, w), lambda t: (t, 0)),
        out_shape=jax.ShapeDtypeStruct((n_t * rows, w), src.dtype),
        scratch_shapes=[pltpu.SemaphoreType.DMA(())],
        compiler_params=_params(("arbitrary",), 4 * _nbytes((rows, w), src.dtype)),
        name="gather_rows",
    )(idx.reshape(n_t, 1, rows), src)


def _moe_up_kernel(te_ref, x_ref, wg_ref, wu_ref, o_ref, xb_ref):
    half = x_ref.shape[1]

    @pl.when(pl.program_id(1) == 0)
    def _():
        lo, hi = _unpack_halves(x_ref[...])
        xb_ref[:, :half] = lo.astype(BF16)
        xb_ref[:, half:] = hi.astype(BF16)

    x = xb_ref[...]
    g = jnp.dot(x, wg_ref[...], preferred_element_type=F32)
    u = jnp.dot(x, wu_ref[...], preferred_element_type=F32)
    o_ref[...] = (g * jax.nn.sigmoid(g) * u).astype(o_ref.dtype)


def _moe_up(xs, wg, wu, tile_expert):
    n_rows, half = xs.shape
    _, d, f = wg.shape
    rows = _MOE_ROWS
    tn = _divisor_tile(f, 512, _LANE)
    w_spec = pl.BlockSpec((None, d, tn), lambda t, j, te: (te[t], 0, j))
    vmem = 2 * (_nbytes((rows, half), jnp.uint32) + 2 * _nbytes((d, tn), BF16) + _nbytes((rows, tn), BF16)) \
        + _nbytes((rows, d), BF16) + 4 * _nbytes((rows, tn), F32)
    return pl.pallas_call(
        _moe_up_kernel,
        grid_spec=pltpu.PrefetchScalarGridSpec(
            num_scalar_prefetch=1,
            grid=(n_rows // rows, f // tn),
            in_specs=[pl.BlockSpec((rows, half), lambda t, j, te: (t, 0)), w_spec, w_spec],
            out_specs=pl.BlockSpec((rows, tn), lambda t, j, te: (t, j)),
            scratch_shapes=[pltpu.VMEM((rows, d), BF16)]),
        out_shape=jax.ShapeDtypeStruct((n_rows, f), BF16),
        compiler_params=_params(("arbitrary", "arbitrary"), vmem + (4 << 20)),
        name="moe_up",
    )(tile_expert, xs, wg, wu)


def _moe_down_kernel(te_ref, a_ref, wlo_ref, whi_ref, o_ref):
    a = a_ref[...]
    lo = jnp.dot(a, wlo_ref[...], preferred_element_type=F32)
    hi = jnp.dot(a, whi_ref[...], preferred_element_type=F32)
    o_ref[...] = _pack_halves(lo, hi)


def _moe_down(act, wd, tile_expert):
    n_rows, f = act.shape
    d = wd.shape[2]
    half = d // 2
    rows = _MOE_ROWS
    tn = _divisor_tile(half, 512, _LANE)
    nj = half // tn
    vmem = 2 * (_nbytes((rows, f), BF16) + 2 * _nbytes((f, tn), BF16) + _nbytes((rows, tn), jnp.uint32)) \
        + 4 * _nbytes((rows, tn), F32)
    return pl.pallas_call(
        _moe_down_kernel,
        grid_spec=pltpu.PrefetchScalarGridSpec(
            num_scalar_prefetch=1,
            grid=(n_rows // rows, nj),
            in_specs=[pl.BlockSpec((rows, f), lambda t, j, te: (t, 0)),
                      pl.BlockSpec((None, f, tn), lambda t, j, te: (te[t], 0, j)),
                      pl.BlockSpec((None, f, tn), lambda t, j, te: (te[t], 0, nj + j))],
            out_specs=pl.BlockSpec((rows, tn), lambda t, j, te: (t, j))),
        out_shape=jax.ShapeDtypeStruct((n_rows, half), jnp.uint32),
        compiler_params=_params(("arbitrary", "arbitrary"), vmem + (4 << 20)),
        name="moe_down",
    )(tile_expert, act, wd, wd)


def _moe_combine_kernel(i1_ref, i2_ref, y_hbm, r_ref, h_ref, g_ref, b_ref, hf_ref, hb_ref,
                        ya_ref, yb_ref, sem, *, rows, alpha):
    def issue(r, carry):
        _row_copy(y_hbm, i1_ref[0, r], ya_ref, r, sem).start(priority=0)
        _row_copy(y_hbm, i2_ref[0, r], yb_ref, r, sem).start(priority=1)
        return carry

    lax.fori_loop(0, rows, issue, 0)
    _wait_rows(y_hbm, ya_ref, sem)
    _wait_rows(y_hbm, yb_ref, sem)

    half = ya_ref.shape[1]
    w1 = r_ref[:, 0:1]
    w2 = r_ref[:, 1:2]
    a_lo, a_hi = _unpack_halves(ya_ref[...])
    b_lo, b_hi = _unpack_halves(yb_ref[...])
    z_lo = alpha * h_ref[:, :half] + (w1 * a_lo + w2 * b_lo)
    z_hi = alpha * h_ref[:, half:] + (w1 * a_hi + w2 * b_hi)
    inv_d = 1.0 / (2 * half)
    mu = (jnp.sum(z_lo, axis=-1, keepdims=True) + jnp.sum(z_hi, axis=-1, keepdims=True)) * inv_d
    c_lo = z_lo - mu
    c_hi = z_hi - mu
    var = (jnp.sum(c_lo * c_lo, axis=-1, keepdims=True) + jnp.sum(c_hi * c_hi, axis=-1, keepdims=True)) * inv_d
    rs = lax.rsqrt(var + LN_EPS)
    y_lo = c_lo * rs * g_ref[:, :half] + b_ref[:, :half]
    y_hi = c_hi * rs * g_ref[:, half:] + b_ref[:, half:]
    hf_ref[:, :half] = y_lo
    hf_ref[:, half:] = y_hi
    hb_ref[:, :half] = y_lo.astype(BF16)
    hb_ref[:, half:] = y_hi.astype(BF16)


def _moe_combine(ypk, dest1, dest2, top2, h, g, b, alpha):
    m, d = h.shape
    rows = _divisor_tile(m, _COMBINE_ROWS, _SUBLANE)
    n_t = m // rows
    idx_spec = pl.BlockSpec((None, 1, rows), lambda t: (t, 0, 0), memory_space=pltpu.SMEM)
    row = pl.BlockSpec((rows, d), lambda t: (t, 0))
    vec = pl.BlockSpec((1, d), lambda t: (0, 0))
    vmem = 2 * (_nbytes((rows, d), F32) * 2 + _nbytes((rows, d), BF16)) + 2 * _nbytes((rows, d // 2), F32) \
        + 6 * _nbytes((rows, d), F32)
    return pl.pallas_call(
        functools.partial(_moe_combine_kernel, rows=rows, alpha=alpha),
        grid=(n_t,),
        in_specs=[idx_spec, idx_spec, pl.BlockSpec(memory_space=pl.ANY),
                  pl.BlockSpec((rows, _LANE), lambda t: (t, 0)), row, vec, vec],
        out_specs=[row, row],
        out_shape=[jax.ShapeDtypeStruct((m, d), F32), jax.ShapeDtypeStruct((m, d), BF16)],
        scratch_shapes=[pltpu.VMEM((rows, d // 2), jnp.uint32), pltpu.VMEM((rows, d // 2), jnp.uint32),
                        pltpu.SemaphoreType.DMA(())],
        compiler_params=_params(("arbitrary",), vmem),
        name="moe_combine",
    )(dest1.reshape(n_t, 1, rows), dest2.reshape(n_t, 1, rows), ypk, top2, h,
      g.reshape(1, d).astype(F32), b.reshape(1, d).astype(F32))


def _moe_routing(top2, n_exp, rows):
    m = top2.shape[0]
    choice = top2[:, 2:4].astype(jnp.int32).reshape(-1)
    onehot = (choice[:, None] == jnp.arange(n_exp, dtype=jnp.int32)[None, :]).astype(jnp.int32)
    running = jnp.cumsum(onehot, axis=0)
    rank = jnp.sum(onehot * running, axis=1) - 1
    counts = running[-1]
    padded = ((counts + rows - 1) // rows) * rows
    ends = jnp.cumsum(padded)
    starts = ends - padded
    n_tiles = -(-2 * m // rows) + n_exp
    dest = jnp.clip(jnp.sum(onehot * starts[None, :], axis=1) + rank, 0, n_tiles * rows - 1)
    src_tok = jnp.zeros((n_tiles * rows,), jnp.int32).at[dest].set(
        jnp.arange(2 * m, dtype=jnp.int32) // 2)
    tile_start = jnp.arange(n_tiles, dtype=jnp.int32) * rows
    tile_expert = jnp.minimum(jnp.sum((tile_start[:, None] >= ends[None, :]).astype(jnp.int32), axis=1),
                              n_exp - 1)
    return src_tok, tile_expert, dest[0::2], dest[1::2]


def _attn_kernel(slopes_ref, q_ref, k_ref, v_ref, lq1_ref, lk1_ref, lq2_ref, lk2_ref, g_ref,
                 o_ref, qqt_ref, vt_ref, s_ref, m_ref, l_ref, acc_ref, *, tq, tk, qk_dim, scale, lam_init):
    h = pl.program_id(1)
    i = pl.program_id(2)
    slope = slopes_ref[h]
    r = tq // tk
    lp = k_ref.shape[0]
    vdim = v_ref.shape[1]

    @pl.when(i == 0)
    def _():
        def xpose(c, carry):
            c0 = pl.multiple_of(c * tk, tk)
            vt_ref[:, pl.ds(c0, tk)] = v_ref[pl.ds(c0, tk), :].astype(F32).T.astype(BF16)
            return carry
        lax.fori_loop(0, lp // tk, xpose, 0)

    log2e = math.log2(math.e)
    qs = q_ref[...].astype(F32) * (scale * log2e)
    lane = lax.broadcasted_iota(jnp.int32, qs.shape, 1)
    ones = jnp.where(lane < 3, 1.0, 0.0).astype(F32)
    qa1 = jnp.concatenate([jnp.where(lane < qk_dim, qs, 0.0), ones], axis=1)
    qa2 = jnp.concatenate([jnp.where(lane >= qk_dim, qs, 0.0), ones], axis=1)
    qqt_ref[:, :tq] = qa1.T.astype(BF16)
    qqt_ref[:, tq:] = qa2.T.astype(BF16)
    slope2 = slope * log2e
    kj_idx = lax.broadcasted_iota(jnp.int32, (tk, vdim), 0).astype(F32)
    kl_idx = lax.broadcasted_iota(jnp.int32, (tk, vdim), 1)
    bias = slope2 * kj_idx
    b_hi = bias.astype(BF16).astype(F32)
    rem = bias - b_hi
    b_mid = rem.astype(BF16).astype(F32)
    b_lo = rem - b_mid
    kb = jnp.where(kl_idx == 0, b_hi, jnp.where(kl_idx == 1, b_mid, jnp.where(kl_idx == 2, b_lo, 0.0)))
    kb = kb.astype(BF16)

    m_ref[...] = jnp.full(m_ref.shape, _NEG, F32)
    l_ref[...] = jnp.zeros(l_ref.shape, F32)
    acc_ref[...] = jnp.zeros(acc_ref.shape, F32)

    def put_scores(slot, j):
        k0 = pl.multiple_of(j * tk, tk)
        ka = jnp.concatenate([k_ref[pl.ds(k0, tk), :], kb], axis=1)
        for half in range(2):
            s_ref[slot, half] = jnp.dot(ka, qqt_ref[:, half * tq:(half + 1) * tq],
                                        preferred_element_type=F32)

    def absorb(j, slot, masked):
        k0 = pl.multiple_of(j * tk, tk)
        vt = vt_ref[:, pl.ds(k0, tk)]
        off = j * tk - i * tq
        c = slope2 * off.astype(F32)
        for half in range(2):
            s = s_ref[slot, half]
            if masked:
                krow = lax.broadcasted_iota(jnp.int32, s.shape, 0)
                qcol = lax.broadcasted_iota(jnp.int32, s.shape, 1)
                s = jnp.where(krow - qcol <= -off, s, _NEG)
            m_old = m_ref[half]
            m_new = jnp.maximum(m_old, jnp.max(s, axis=0, keepdims=True) + c)
            p = jnp.exp2(s + (c - m_new))
            alpha = jnp.exp2(m_old - m_new)
            l_ref[half] = alpha * l_ref[half] + jnp.sum(p, axis=0, keepdims=True)
            pv = jnp.dot(vt, p.astype(BF16), preferred_element_type=F32)
            acc_ref[half] = alpha * acc_ref[half] + pv
            m_ref[half] = m_new

    n_full = i * r
    odd = n_full & 1

    @pl.when(odd == 1)
    def _():
        put_scores(0, 0)
        absorb(0, 0, False)

    put_scores(0, odd)

    def pair(t, carry):
        j = odd + 2 * t
        put_scores(1, j + 1)
        absorb(j, 0, False)
        put_scores(0, j + 2)
        absorb(j + 1, 1, False)
        return carry

    lax.fori_loop(0, lax.shift_right_logical(n_full - odd, 1), pair, 0)
    for d in range(r):
        if d + 1 < r:
            put_scores((d + 1) % 2, n_full + d + 1)
        absorb(n_full + d, d % 2, True)

    lam = (jnp.exp(jnp.sum(lq1_ref[...] * lk1_ref[...], axis=1, keepdims=True))
           - jnp.exp(jnp.sum(lq2_ref[...] * lk2_ref[...], axis=1, keepdims=True)) + lam_init)
    ot = acc_ref[0] * (1.0 / l_ref[0]) - lam * (acc_ref[1] * (1.0 / l_ref[1]))
    ms = jnp.mean(ot * ot, axis=0, keepdims=True)
    y = ot * lax.rsqrt(ms + LN_EPS) * g_ref[...] * (1.0 - lam_init)
    o_ref[...] = y.T.astype(o_ref.dtype)


def _diff_attention(proj, bsz, lp, d_att, lam_vecs, subln_g, lam_init):
    vdim = d_att // ATT_HEADS
    qk_dim = vdim // 2
    assert vdim == _LANE
    tq = _divisor_tile(lp, _TS_TARGET, _SEQ_ALIGN)
    tk = _divisor_tile(tq, _TK_ATT, _LANE)
    nq = lp // tq
    slopes = jnp.asarray(2.0 ** (-8.0 * np.arange(1, ATT_HEADS + 1) / ATT_HEADS), F32)
    lq1, lk1, lq2, lk2 = [v.reshape(1, qk_dim).astype(F32) for v in lam_vecs]
    vec = pl.BlockSpec((1, qk_dim), lambda b, h, i: (0, 0))
    in_specs = [
        pl.BlockSpec(memory_space=pltpu.SMEM),
        pl.BlockSpec((tq, vdim), lambda b, h, i: (b * nq + i, h)),
        pl.BlockSpec((lp, vdim), lambda b, h, i: (b, ATT_HEADS + h)),
        pl.BlockSpec((lp, vdim), lambda b, h, i: (b, 2 * ATT_HEADS + h)),
        vec, vec, vec, vec,
        pl.BlockSpec((vdim, 1), lambda b, h, i: (0, 0)),
    ]
    vmem = 2 * (2 * _nbytes((lp, vdim), BF16) + 2 * _nbytes((tq, vdim), BF16)) \
        + _nbytes((lp, vdim), BF16) + 4 * _nbytes((2 * tq, vdim), F32) + 6 * _nbytes((2 * tq, tk), F32)
    return pl.pallas_call(
        functools.partial(_attn_kernel, tq=tq, tk=tk, qk_dim=qk_dim, scale=qk_dim ** -0.5,
                          lam_init=lam_init),
        grid=(bsz, ATT_HEADS, nq),
        in_specs=in_specs,
        out_specs=pl.BlockSpec((tq, vdim), lambda b, h, i: (b * nq + i, h)),
        out_shape=jax.ShapeDtypeStruct((bsz * lp, d_att), BF16),
        scratch_shapes=[pltpu.VMEM((2 * vdim, 2 * tq), BF16), pltpu.VMEM((vdim, lp), BF16),
                        pltpu.VMEM((2, 2, tk, tq), F32),
                        pltpu.VMEM((2, 1, tq), F32), pltpu.VMEM((2, 1, tq), F32),
                        pltpu.VMEM((2, vdim, tq), F32)],
        compiler_params=_params(("parallel", "parallel", "arbitrary"), vmem + (8 << 20)),
        name="diff_attention",
    )(slopes, proj, proj, proj, lq1, lk1, lq2, lk2, subln_g.reshape(vdim, 1).astype(F32))


_CONV_HALO = 32
_CONV_ROWS = 32


def _conv_kernel(a_ref, gate_ref, dw_ref, dwb_ref, lng_ref, lnb_ref, pw_ref, pwb_ref, o_ref,
                 hbuf_ref, cbuf_ref, sh_ref, *, ts, k_len):
    t = pl.program_id(1)
    c_dim = a_ref.shape[1]

    @pl.when(t == 0)
    def _():
        hbuf_ref[0:_CONV_HALO, :] = jnp.zeros((_CONV_HALO, c_dim), F32)

    @pl.when(t > 0)
    def _():
        hbuf_ref[0:_CONV_HALO, :] = hbuf_ref[ts:ts + _CONV_HALO, :]

    a = a_ref[...].astype(F32)
    hbuf_ref[_CONV_HALO:_CONV_HALO + ts, :] = a * jax.nn.sigmoid(gate_ref[...].astype(F32))

    first = _CONV_HALO - (k_len - 1)
    sh_rows = sh_ref.shape[1]
    for lb in range(c_dim // _LANE):
        lanes = slice(lb * _LANE, (lb + 1) * _LANE)
        for b in range(1, _SUBLANE):
            sh_ref[b - 1] = hbuf_ref[b:b + sh_rows, lanes]
        taps = [dw_ref[k:k + 1, lanes] for k in range(k_len)]
        bias = dwb_ref[:, lanes]

        def chunk(c, carry, lanes=lanes, taps=taps, bias=bias):
            base = c * _CONV_ROWS
            acc = jnp.broadcast_to(bias, (_CONV_ROWS, _LANE))
            for k in range(k_len):
                a, b = divmod(first + k, _SUBLANE)
                start = pl.multiple_of(base + a * _SUBLANE, _SUBLANE)
                if b == 0:
                    src = hbuf_ref[pl.ds(start, _CONV_ROWS), lanes]
                else:
                    src = sh_ref[b - 1, pl.ds(start, _CONV_ROWS), :]
                acc = acc + taps[k] * src
            cbuf_ref[pl.ds(pl.multiple_of(base, _CONV_ROWS), _CONV_ROWS), lanes] = acc
            return carry

        lax.fori_loop(0, ts // _CONV_ROWS, chunk, 0)

    y = cbuf_ref[...]
    mu = jnp.mean(y, axis=-1, keepdims=True)
    yc = y - mu
    var = jnp.mean(yc * yc, axis=-1, keepdims=True)
    yn = yc * lax.rsqrt(var + LN_EPS) * lng_ref[...] + lnb_ref[...]
    act = (yn * jax.nn.sigmoid(yn)).astype(BF16)
    out = jnp.dot(act, pw_ref[...], preferred_element_type=F32) + pwb_ref[...]
    o_ref[...] = out.astype(o_ref.dtype)


def _conformer_conv(proj, bsz, lp, col0, d_conv, dw_w, dw_b, ln_g, ln_b, pw_w, pw_b):
    ts = _divisor_tile(lp, _TS_TARGET, _CONV_ROWS)
    nt = lp // ts
    cb = col0 // d_conv
    assert col0 % d_conv == 0 and dw_w.shape[0] - 1 <= _CONV_HALO
    vec = pl.BlockSpec((1, d_conv), lambda b, t: (0, 0))
    in_specs = [
        pl.BlockSpec((ts, d_conv), lambda b, t: (b * nt + t, cb)),
        pl.BlockSpec((ts, d_conv), lambda b, t: (b * nt + t, cb + 1)),
        pl.BlockSpec((dw_w.shape[0], d_conv), lambda b, t: (0, 0)),
        vec, vec, vec,
        pl.BlockSpec((d_conv, d_conv), lambda b, t: (0, 0)),
        vec,
    ]
    vmem = 4 * _nbytes((ts, d_conv), BF16) * 2 + 8 * _nbytes((ts + _CONV_HALO, d_conv), F32)
    r1 = lambda v: v.reshape(1, d_conv).astype(F32)
    return pl.pallas_call(
        functools.partial(_conv_kernel, ts=ts, k_len=dw_w.shape[0]),
        grid=(bsz, nt),
        in_specs=in_specs,
        out_specs=pl.BlockSpec((ts, d_conv), lambda b, t: (b * nt + t, 0)),
        out_shape=jax.ShapeDtypeStruct((bsz * lp, d_conv), BF16),
        scratch_shapes=[pltpu.VMEM((ts + _CONV_HALO, d_conv), F32), pltpu.VMEM((ts, d_conv), F32),
                        pltpu.VMEM((_SUBLANE - 1, ts + _CONV_HALO - _SUBLANE, _LANE), F32)],
        compiler_params=_params(("parallel", "arbitrary"), vmem),
        name="conformer_conv",
    )(proj, proj, dw_w.astype(F32), r1(dw_b), r1(ln_g), r1(ln_b), pw_w.astype(BF16), r1(pw_b))


_POOL_HALO = 16


def _pool_kernel(u_ref, w_ref, b_ref, s_ref, o_ref, buf_ref, *, ts, windows, gw):
    t = pl.program_id(1)
    c_dim = u_ref.shape[1]

    @pl.when(t == 0)
    def _():
        buf_ref[0:_POOL_HALO, :] = jnp.zeros((_POOL_HALO, c_dim), F32)

    @pl.when(t > 0)
    def _():
        buf_ref[0:_POOL_HALO, :] = buf_ref[ts:ts + _POOL_HALO, :]

    buf_ref[_POOL_HALO:_POOL_HALO + ts, :] = u_ref[...].astype(F32)
    pos = (t * ts + 1 + lax.broadcasted_iota(jnp.int32, (ts, 1), 0)).astype(F32)
    for gi, win in enumerate(windows):
        lanes = slice(gi * gw, (gi + 1) * gw)
        u = buf_ref[_POOL_HALO:_POOL_HALO + ts, lanes]
        tot = u
        for j in range(1, win):
            tot = tot + buf_ref[_POOL_HALO - j:_POOL_HALO - j + ts, lanes]
        d = tot / jnp.minimum(pos, float(win)) - u
        y = jnp.dot(d.astype(BF16), w_ref[gi], preferred_element_type=F32)
        o_ref[:, lanes] = ((y + b_ref[:, lanes]) * s_ref[:, lanes]).astype(o_ref.dtype)


def _multiscale_pool(proj, bsz, lp, col0, d_pool, w, bias, scale):
    n_g = len(POOL_WINDOWS)
    gw = d_pool // n_g
    ts = _divisor_tile(lp, _TS_TARGET, _SUBLANE)
    nt = lp // ts
    assert col0 % d_pool == 0 and max(POOL_WINDOWS) - 1 <= _POOL_HALO
    cb = col0 // d_pool
    vec = pl.BlockSpec((1, d_pool), lambda b, t: (0, 0))
    r1 = lambda v: v.reshape(1, d_pool).astype(F32)
    return pl.pallas_call(
        functools.partial(_pool_kernel, ts=ts, windows=POOL_WINDOWS, gw=gw),
        grid=(bsz, nt),
        in_specs=[pl.BlockSpec((ts, d_pool), lambda b, t: (b * nt + t, cb)),
                  pl.BlockSpec((n_g, gw, gw), lambda b, t: (0, 0, 0)), vec, vec],
        out_specs=pl.BlockSpec((ts, d_pool), lambda b, t: (b * nt + t, 0)),
        out_shape=jax.ShapeDtypeStruct((bsz * lp, d_pool), BF16),
        scratch_shapes=[pltpu.VMEM((ts + _POOL_HALO, d_pool), F32)],
        compiler_params=_params(("parallel", "arbitrary"), 10 * _nbytes((ts, d_pool), F32)),
        name="multiscale_pool",
    )(proj, w.astype(BF16), r1(bias), r1(scale))


_SSD_HALO = 8


def _bf16_terms(x):
    hi = x.astype(BF16)
    rem = x - hi.astype(F32)
    mid = rem.astype(BF16)
    lo = (rem - mid.astype(F32)).astype(BF16)
    return hi, mid, lo


def _ssd_kernel(z_ref, x_ref, bc_ref, dt_ref, cw_ref, cb_ref, dtb_ref, alog_ref, dskip_ref, ng_ref,
                expand_ref, o_ref, ext_ref, state_ref, *, tc, d_ssm, k_len):
    c = pl.program_id(1)
    n_state = SSM_STATE
    gwid = d_ssm // SSM_GROUPS
    hpg = gwid // SSM_HEADDIM
    xbc_dim = x_ref.shape[1] + bc_ref.shape[1]

    @pl.when(c == 0)
    def _():
        ext_ref[0:_SSD_HALO, :] = jnp.zeros((_SSD_HALO, xbc_dim), F32)
        state_ref[...] = jnp.zeros(state_ref.shape, F32)

    @pl.when(c > 0)
    def _():
        ext_ref[0:_SSD_HALO, :] = ext_ref[tc:tc + _SSD_HALO, :]

    ext_ref[_SSD_HALO:_SSD_HALO + tc, :d_ssm] = x_ref[...].astype(F32)
    ext_ref[_SSD_HALO:_SSD_HALO + tc, d_ssm:] = bc_ref[...].astype(F32)

    first = _SSD_HALO - (k_len - 1)
    pre = jnp.broadcast_to(cb_ref[...], (tc, xbc_dim))
    for k in range(k_len):
        pre = pre + cw_ref[k:k + 1, :] * ext_ref[first + k:first + k + tc, :]
    xbc = pre * jax.nn.sigmoid(pre)
    xs = xbc[:, :d_ssm]
    bm = xbc[:, d_ssm:d_ssm + SSM_GROUPS * n_state].astype(BF16)
    cm = xbc[:, d_ssm + SSM_GROUPS * n_state:].astype(BF16)

    dt = jax.nn.softplus(dt_ref[...] + dtb_ref[...])
    a = dt * (-jnp.exp(alog_ref[...]))
    ti = lax.broadcasted_iota(jnp.int32, (tc, tc), 0)
    si = lax.broadcasted_iota(jnp.int32, (tc, tc), 1)
    causal = ti >= si
    tri = jnp.where(causal, 1.0, 0.0).astype(BF16)
    a_cs = sum(jnp.dot(tri, part, preferred_element_type=F32) for part in _bf16_terms(a))
    a_cs_t = a_cs.T
    expand = expand_ref[...]
    a_cs_x = sum(jnp.dot(part, expand, preferred_element_type=F32) for part in _bf16_terms(a_cs))
    dt_x = sum(jnp.dot(part, expand, preferred_element_type=F32) for part in _bf16_terms(dt))
    a_last = a_cs_x[tc - 1:tc, :]
    x_dt = xs * dt_x
    x_end = (x_dt * jnp.exp(a_last - a_cs_x)).astype(BF16)
    x_dt_b = x_dt.astype(BF16)
    decay_in = jnp.exp(a_cs_x)
    decay_chunk = jnp.exp(a_last)

    lane_g = lax.broadcasted_iota(jnp.int32, (tc, gwid), 1)
    ys = []
    for g in range(SSM_GROUPS):
        cg = cm[:, g * n_state:(g + 1) * n_state]
        bg = bm[:, g * n_state:(g + 1) * n_state]
        gl = slice(g * gwid, (g + 1) * gwid)
        cb = lax.dot_general(cg, bg, (((1,), (1,)), ((), ())), preferred_element_type=F32)
        xg = x_dt_b[:, gl]
        y_g = None
        for r in range(hpg):
            hd = g * hpg + r
            seg = a_cs[:, hd:hd + 1] - a_cs_t[hd:hd + 1, :]
            lmat = jnp.exp(jnp.where(causal, seg, _NEG))
            mh = (cb * lmat).astype(BF16)
            in_head = (lane_g >= r * SSM_HEADDIM) & (lane_g < (r + 1) * SSM_HEADDIM)
            xh = jnp.where(in_head, xg, jnp.zeros_like(xg))
            yh = jnp.dot(mh, xh, preferred_element_type=F32)
            y_g = yh if y_g is None else y_g + yh
        st = state_ref[g]
        y_g = y_g + jnp.dot(cg, st.astype(BF16), preferred_element_type=F32) * decay_in[:, gl]
        upd = lax.dot_general(bg, x_end[:, gl], (((0,), (0,)), ((), ())), preferred_element_type=F32)
        state_ref[g] = decay_chunk[:, gl] * st + upd
        ys.append(y_g)
    y = jnp.concatenate(ys, axis=1) + dskip_ref[...] * xs
    zf = z_ref[...].astype(F32)
    y = y * (zf * jax.nn.sigmoid(zf))
    outs = []
    for g in range(SSM_GROUPS):
        yg = y[:, g * gwid:(g + 1) * gwid]
        ms = jnp.mean(yg * yg, axis=-1, keepdims=True)
        outs.append(yg * lax.rsqrt(ms + LN_EPS))
    o_ref[...] = (jnp.concatenate(outs, axis=1) * ng_ref[...]).astype(o_ref.dtype)


def _mamba2_ssd(proj, dt_raw, bsz, lp, z_col0, d_ssm, conv_w, conv_b, dt_bias, a_log, d_skip, norm_g):
    xbc_dim = d_ssm + 2 * SSM_GROUPS * SSM_STATE
    n_heads = d_ssm // SSM_HEADDIM
    tc = _SSD_CHUNK
    nc = lp // tc
    bc_dim = xbc_dim - d_ssm
    assert lp % tc == 0 and z_col0 % d_ssm == 0 and (z_col0 + 2 * d_ssm) % bc_dim == 0
    zb = z_col0 // d_ssm
    bcb = (z_col0 + 2 * d_ssm) // bc_dim
    pad_h = lambda v: jnp.pad(v.astype(F32).reshape(1, n_heads), ((0, 0), (0, _LANE - n_heads)))
    expand = (np.arange(d_ssm)[None, :] // SSM_HEADDIM == np.arange(_LANE)[:, None]).astype(np.float32)
    d_x = jnp.repeat(d_skip.astype(F32), SSM_HEADDIM).reshape(1, d_ssm)
    full = lambda shape: pl.BlockSpec(shape, lambda b, c: (0,) * len(shape))
    in_specs = [
        pl.BlockSpec((tc, d_ssm), lambda b, c: (b * nc + c, zb)),
        pl.BlockSpec((tc, d_ssm), lambda b, c: (b * nc + c, zb + 1)),
        pl.BlockSpec((tc, bc_dim), lambda b, c: (b * nc + c, bcb)),
        pl.BlockSpec((tc, _LANE), lambda b, c: (b * nc + c, 0)),
        full((conv_w.shape[0], xbc_dim)), full((1, xbc_dim)),
        full((1, _LANE)), full((1, _LANE)), full((1, d_ssm)), full((1, d_ssm)),
        full((_LANE, d_ssm)),
    ]
    return pl.pallas_call(
        functools.partial(_ssd_kernel, tc=tc, d_ssm=d_ssm, k_len=conv_w.shape[0]),
        grid=(bsz, nc),
        in_specs=in_specs,
        out_specs=pl.BlockSpec((tc, d_ssm), lambda b, c: (b * nc + c, 0)),
        out_shape=jax.ShapeDtypeStruct((bsz * lp, d_ssm), BF16),
        scratch_shapes=[pltpu.VMEM((tc + _SSD_HALO, xbc_dim), F32),
                        pltpu.VMEM((SSM_GROUPS, SSM_STATE, d_ssm // SSM_GROUPS), F32)],
        compiler_params=_params(("parallel", "arbitrary"), 32 << 20),
        name="mamba2_ssd",
    )(proj, proj, proj, dt_raw, conv_w.astype(F32), conv_b.reshape(1, xbc_dim).astype(F32),
      pad_h(dt_bias), pad_h(a_log), d_x, norm_g.reshape(1, d_ssm).astype(F32),
      jnp.asarray(expand, dtype=BF16))


def kernel(x, meta, emb_ln_g, emb_ln_b, w_in, lam_q1, lam_k1, lam_q2, lam_k2, att_subln_g, cf_dw_w, cf_dw_b, cf_ln_g, cf_ln_b, cf_pw_w, cf_pw_b, pool_w, pool_b, pool_scale, ssm_conv_w, ssm_conv_b, ssm_dt_bias, ssm_a_log, ssm_d, ssm_norm_g, w_out, ln1_g, ln1_b, ln2_g, ln2_b, ffn_w_gate, ffn_w_up, ffn_w_down, moe_router, moe_w_gate, moe_w_up, moe_w_down):
    bsz, seq, d_model = x.shape
    depth = w_in.shape[0]
    n_meta = meta.shape[0]
    d_att = d_conv = d_pool = d_ssm = d_model // 4
    xbc_dim = d_ssm + 2 * SSM_GROUPS * SSM_STATE
    n_heads = d_ssm // SSM_HEADDIM
    d_main = 3 * d_att + 2 * d_conv + d_pool + d_ssm + xbc_dim
    alpha = (2 * depth) ** 0.25

    seq_len = n_meta + seq
    lp = -(-seq_len // _SEQ_ALIGN) * _SEQ_ALIGN
    m = bsz * lp
    h, h16 = _embed_layer_norm(x, meta, emb_ln_g, emb_ln_b, lp)

    for l in range(depth):
        proj = _matmul([h16], _weight_bf16(w_in, l, 1, d_main)[0], BF16)
        w_dt = jnp.pad(w_in[l, :, d_main:], ((0, 0), (0, _LANE - n_heads))).astype(BF16)
        dt_raw = _matmul([h16], w_dt, F32)
        lam_init = 0.8 - 0.6 * math.exp(-0.3 * l)

        y_att = _diff_attention(proj, bsz, lp, d_att, (lam_q1[l], lam_k1[l], lam_q2[l], lam_k2[l]),
                                att_subln_g[l], lam_init)
        y_conv = _conformer_conv(proj, bsz, lp, 3 * d_att, d_conv, cf_dw_w[l], cf_dw_b[l],
                                 cf_ln_g[l], cf_ln_b[l], cf_pw_w[l], cf_pw_b[l])
        y_pool = _multiscale_pool(proj, bsz, lp, 3 * d_att + 2 * d_conv, d_pool,
                                  pool_w[l], pool_b[l], pool_scale[l])
        y_ssm = _mamba2_ssd(proj, dt_raw, bsz, lp, 3 * d_att + 2 * d_conv + d_pool, d_ssm,
                            ssm_conv_w[l], ssm_conv_b[l], ssm_dt_bias[l], ssm_a_log[l], ssm_d[l],
                            ssm_norm_g[l])
        z = _matmul([y_att, y_conv, y_pool, y_ssm], _weight_bf16(w_out, l, 1)[0], F32,
                    residual=h, alpha=alpha)

        e = l // 2
        if l % 2 == 0:
            h, h16 = _layer_norm(z, ln1_g[l], ln1_b[l])
            act = _swiglu_up(h16, _weight_bf16(ffn_w_gate, e, 1)[0], _weight_bf16(ffn_w_up, e, 1)[0])
            z = _matmul([act], _weight_bf16(ffn_w_down, e, 1)[0], F32, residual=h, alpha=alpha)
            h, h16 = _layer_norm(z, ln2_g[l], ln2_b[l])
        else:
            n_exp = moe_router.shape[2]
            expert_bf16 = lambda w: _weight_bf16(w.reshape((-1,) + w.shape[2:]), e * n_exp, n_exp)
            h, h16, h_packed = _layer_norm(z, ln1_g[l], ln1_b[l], packed=True)
            top2 = _router_top2(h, moe_router[e])
            src_tok, tile_expert, dest1, dest2 = _moe_routing(top2, n_exp, _MOE_ROWS)
            xs = _gather_rows(h_packed, src_tok, _MOE_ROWS)
            act = _moe_up(xs, expert_bf16(moe_w_gate), expert_bf16(moe_w_up), tile_expert)
            y_sorted = _moe_down(act, expert_bf16(moe_w_down), tile_expert)
            h, h16 = _moe_combine(y_sorted, dest1, dest2, top2, h, ln2_g[l], ln2_b[l], alpha)

    return h.reshape(bsz, lp, d_model)[:, n_meta:seq_len]
```

```python
import functools
import math

import numpy as np
import jax
import jax.numpy as jnp
from jax import lax
from jax.experimental import pallas as pl
from jax.experimental.pallas import tpu as pltpu

F32 = jnp.float32
BF16 = jnp.bfloat16
HIGHEST = lax.Precision.HIGHEST

N_META = 16
ATT_HEADS = 8
CONV_K = 31
POOL_WINDOWS = (2, 4, 8, 16)
SSM_HEADDIM = 64
SSM_GROUPS = 4
SSM_STATE = 128
SSM_CONV_K = 4
N_EXPERTS = 8
LN_EPS = 1e-5

_LANE = 128
_SUBLANE = 8
_VMEM_CAP = 60 * 1024 * 1024

_SEQ_ALIGN = 256
_TM_TARGET = 1536
_TS_TARGET = 768
_TK_ATT = 256
_SSD_CHUNK = 128
_LN_ROWS = 256
_NEG = -1e30


def _divisor_tile(n, target, align):
    best = align
    t = align
    while t <= min(n, target):
        if n % t == 0:
            best = t
        t += align
    assert n % best == 0, (n, target, align)
    return best


def _params(sem, vmem_bytes):
    return pltpu.CompilerParams(
        dimension_semantics=sem,
        vmem_limit_bytes=int(min(max(vmem_bytes, 32 * 1024 * 1024), _VMEM_CAP)))


def _nbytes(shape, dtype):
    return int(np.prod(shape)) * jnp.dtype(dtype).itemsize


def _ln_kernel(z_ref, g_ref, b_ref, hf_ref, hb_ref, *maybe_packed_ref):
    z = z_ref[...]
    mu = jnp.mean(z, axis=-1, keepdims=True)
    zc = z - mu
    var = jnp.mean(zc * zc, axis=-1, keepdims=True)
    y = zc * lax.rsqrt(var + LN_EPS) * g_ref[...] + b_ref[...]
    hf_ref[...] = y
    hb_ref[...] = y.astype(BF16)
    if maybe_packed_ref:
        half = y.shape[1] // 2
        maybe_packed_ref[0][...] = _pack_halves(y[:, :half], y[:, half:])


def _layer_norm(z, g, b, packed=False):
    m, d = z.shape
    tm = _divisor_tile(m, _LN_ROWS, _SUBLANE)
    row = pl.BlockSpec((tm, d), lambda i: (i, 0))
    vec = pl.BlockSpec((1, d), lambda i: (0, 0))
    out_specs = [row, row]
    out_shape = [jax.ShapeDtypeStruct((m, d), F32), jax.ShapeDtypeStruct((m, d), BF16)]
    if packed:
        out_specs.append(pl.BlockSpec((tm, d // 2), lambda i: (i, 0)))
        out_shape.append(jax.ShapeDtypeStruct((m, d // 2), jnp.uint32))
    return pl.pallas_call(
        _ln_kernel,
        grid=(m // tm,),
        in_specs=[row, vec, vec],
        out_specs=out_specs,
        out_shape=out_shape,
        compiler_params=_params(("parallel",), 10 * _nbytes((tm, d), F32)),
        name="layer_norm",
    )(z, g.reshape(1, d).astype(F32), b.reshape(1, d).astype(F32))


def _embed_ln_kernel(prev_ref, cur_ref, meta_ref, g_ref, b_ref, hf_ref, hb_ref, *, rows, seq_len):
    t = pl.program_id(1)
    n_meta = meta_ref.shape[0]
    top = jnp.where(t == 0, meta_ref[...], prev_ref[rows - n_meta:, :])
    tok = jnp.concatenate([top, cur_ref[:rows - n_meta, :]], axis=0)
    pos = t * rows + lax.broadcasted_iota(jnp.int32, (rows, 1), 0)
    z = jnp.where(pos < seq_len, tok, 0.0)
    mu = jnp.mean(z, axis=-1, keepdims=True)
    zc = z - mu
    var = jnp.mean(zc * zc, axis=-1, keepdims=True)
    y = zc * lax.rsqrt(var + LN_EPS) * g_ref[...] + b_ref[...]
    hf_ref[...] = y
    hb_ref[...] = y.astype(BF16)


def _embed_layer_norm(x, meta, g, b, lp):
    bsz, seq, d = x.shape
    n_meta = meta.shape[0]
    rows = _divisor_tile(math.gcd(seq, lp), _LN_ROWS, _SUBLANE)
    assert n_meta % _SUBLANE == 0 and n_meta <= rows
    nx = seq // rows
    nt = lp // rows
    out_row = pl.BlockSpec((rows, d), lambda bi, t: (bi * nt + t, 0))
    vec = pl.BlockSpec((1, d), lambda bi, t: (0, 0))
    return pl.pallas_call(
        functools.partial(_embed_ln_kernel, rows=rows, seq_len=n_meta + seq),
        grid=(bsz, nt),
        in_specs=[pl.BlockSpec((None, rows, d), lambda bi, t: (bi, jnp.clip(t - 1, 0, nx - 1), 0)),
                  pl.BlockSpec((None, rows, d), lambda bi, t: (bi, jnp.minimum(t, nx - 1), 0)),
                  pl.BlockSpec((n_meta, d), lambda bi, t: (0, 0)), vec, vec],
        out_specs=[out_row, out_row],
        out_shape=[jax.ShapeDtypeStruct((bsz * lp, d), F32), jax.ShapeDtypeStruct((bsz * lp, d), BF16)],
        compiler_params=_params(("parallel", "parallel"), 12 * _nbytes((rows, d), F32)),
        name="embed_layer_norm",
    )(x, x, meta.astype(F32), g.reshape(1, d).astype(F32), b.reshape(1, d).astype(F32))


def _cast_kernel(w_ref, o_ref):
    o_ref[...] = w_ref[...].astype(o_ref.dtype)


def _weight_bf16(w, s0, n_s, n_cols=None):
    _, r, c = w.shape
    n_cols = c if n_cols is None else n_cols
    rb = _divisor_tile(r, 512, _SUBLANE)
    cb = _divisor_tile(n_cols, 2048, _LANE)
    return pl.pallas_call(
        _cast_kernel,
        grid=(n_s, r // rb, n_cols // cb),
        in_specs=[pl.BlockSpec((None, rb, cb), lambda s, i, j: (s0 + s, i, j))],
        out_specs=pl.BlockSpec((None, rb, cb), lambda s, i, j: (s, i, j)),
        out_shape=jax.ShapeDtypeStruct((n_s, r, n_cols), BF16),
        compiler_params=_params(("parallel", "parallel", "parallel"), 4 * _nbytes((rb, cb), F32)),
        name="weight_bf16",
    )(w)


def _tail_cast_kernel(w_ref, o_ref, *, n_valid):
    lane = lax.broadcasted_iota(jnp.int32, w_ref.shape, 1)
    o_ref[...] = jnp.where(lane < n_valid, w_ref[...], 0.0).astype(o_ref.dtype)


def _weight_tail_bf16(w, s, col0):
    _, r, c = w.shape
    assert col0 % _LANE == 0 and 0 < c - col0 <= _LANE
    rb = _divisor_tile(r, 512, _SUBLANE)
    return pl.pallas_call(
        functools.partial(_tail_cast_kernel, n_valid=c - col0),
        grid=(r // rb,),
        in_specs=[pl.BlockSpec((None, rb, _LANE), lambda i: (s, i, col0 // _LANE))],
        out_specs=pl.BlockSpec((rb, _LANE), lambda i: (i, 0)),
        out_shape=jax.ShapeDtypeStruct((r, _LANE), BF16),
        compiler_params=_params(("parallel",), 32 << 20),
        name="weight_tail_bf16",
    )(w)


def _mm_kernel(*refs, n_in, alpha, has_res):
    x_refs = refs[:n_in]
    w_refs = refs[n_in:2 * n_in]
    o_ref = refs[-1]
    acc = None
    for x_ref, w_ref in zip(x_refs, w_refs):
        d = jnp.dot(x_ref[...], w_ref[...], preferred_element_type=F32)
        acc = d if acc is None else acc + d
    if has_res:
        acc = acc + alpha * refs[2 * n_in][...]
    o_ref[...] = acc.astype(o_ref.dtype)


def _matmul(xs, w, out_dtype, residual=None, alpha=1.0):
    n_in = len(xs)
    m, kp = xs[0].shape
    k_total, n = w.shape
    assert k_total == n_in * kp
    tm = _divisor_tile(m, _TM_TARGET if k_total <= 4096 else _TM_TARGET // 2, _LANE)
    tn = _divisor_tile(n, 512 if k_total <= 4096 else 256, _LANE)
    if k_total > 8192:
        tm = _divisor_tile(m, _TM_TARGET // 4, _LANE)
    in_specs = [pl.BlockSpec((tm, kp), lambda i, j: (i, 0)) for _ in range(n_in)]
    in_specs += [pl.BlockSpec((kp, tn), lambda i, j, p=p: (p, j)) for p in range(n_in)]
    args = list(xs) + [w] * n_in
    vmem = 2 * (n_in * _nbytes((tm, kp), BF16) + n_in * _nbytes((kp, tn), BF16)
                + _nbytes((tm, tn), out_dtype)) + 2 * _nbytes((tm, tn), F32)
    if residual is not None:
        in_specs.append(pl.BlockSpec((tm, tn), lambda i, j: (i, j)))
        args.append(residual)
        vmem += 2 * _nbytes((tm, tn), F32)
    return pl.pallas_call(
        functools.partial(_mm_kernel, n_in=n_in, alpha=alpha, has_res=residual is not None),
        grid=(m // tm, n // tn),
        in_specs=in_specs,
        out_specs=pl.BlockSpec((tm, tn), lambda i, j: (i, j)),
        out_shape=jax.ShapeDtypeStruct((m, n), out_dtype),
        compiler_params=_params(("parallel", "arbitrary"), vmem + (4 << 20)),
        name="matmul",
    )(*args)


def _glu_kernel(x_ref, wg_ref, wu_ref, o_ref):
    x = x_ref[...]
    g = jnp.dot(x, wg_ref[...], preferred_element_type=F32)
    u = jnp.dot(x, wu_ref[...], preferred_element_type=F32)
    o_ref[...] = (g * jax.nn.sigmoid(g) * u).astype(o_ref.dtype)


def _swiglu_up(x, wg, wu):
    m, d = x.shape
    f = wg.shape[1]
    tm = _divisor_tile(m, _TM_TARGET, _LANE)
    tn = _divisor_tile(f, 256, _LANE)
    w_spec = pl.BlockSpec((d, tn), lambda i, j: (0, j))
    vmem = 2 * (_nbytes((tm, d), BF16) + 2 * _nbytes((d, tn), BF16) + _nbytes((tm, tn), BF16)) \
        + 4 * _nbytes((tm, tn), F32)
    return pl.pallas_call(
        _glu_kernel,
        grid=(m // tm, f // tn),
        in_specs=[pl.BlockSpec((tm, d), lambda i, j: (i, 0)), w_spec, w_spec],
        out_specs=pl.BlockSpec((tm, tn), lambda i, j: (i, j)),
        out_shape=jax.ShapeDtypeStruct((m, f), BF16),
        compiler_params=_params(("parallel", "arbitrary"), vmem + (4 << 20)),
        name="swiglu_up",
    )(x, wg, wu)


def _router_kernel(h_ref, r_ref, o_ref, *, n_exp):
    logits = jnp.dot(h_ref[...], r_ref[...], preferred_element_type=F32, precision=HIGHEST)
    lane = lax.broadcasted_iota(jnp.int32, logits.shape, 1).astype(F32)
    lg = jnp.where(lane < n_exp, logits, _NEG)
    m1 = jnp.max(lg, axis=1, keepdims=True)
    i1 = jnp.min(jnp.where(lg == m1, lane, float(_LANE)), axis=1, keepdims=True)
    lg2 = jnp.where(lane == i1, _NEG, lg)
    m2 = jnp.max(lg2, axis=1, keepdims=True)
    i2 = jnp.min(jnp.where(lg2 == m2, lane, float(_LANE)), axis=1, keepdims=True)
    e = jnp.exp(m2 - m1)
    w1 = 1.0 / (1.0 + e)
    w2 = e / (1.0 + e)
    o_ref[...] = (jnp.where(lane == 0.0, w1, 0.0) + jnp.where(lane == 1.0, w2, 0.0)
                  + jnp.where(lane == 2.0, i1, 0.0) + jnp.where(lane == 3.0, i2, 0.0))


def _router_top2(h, router):
    m, d = h.shape
    n_exp = router.shape[1]
    tm = _divisor_tile(m, 512, _SUBLANE)
    r_pad = jnp.pad(router.astype(F32), ((0, 0), (0, _LANE - n_exp)))
    return pl.pallas_call(
        functools.partial(_router_kernel, n_exp=n_exp),
        grid=(m // tm,),
        in_specs=[pl.BlockSpec((tm, d), lambda i: (i, 0)), pl.BlockSpec((d, _LANE), lambda i: (0, 0))],
        out_specs=pl.BlockSpec((tm, _LANE), lambda i: (i, 0)),
        out_shape=jax.ShapeDtypeStruct((m, _LANE), F32),
        compiler_params=_params(("parallel",), 4 * _nbytes((tm, d), F32) + (8 << 20)),
        name="router",
    )(h, r_pad)


_HI16 = np.uint32(0xFFFF0000)
_MOE_ROWS = 512
_COMBINE_ROWS = 256


def _pack_halves(lo, hi):
    lo_bits = lax.bitcast_convert_type(lo.astype(BF16).astype(F32), jnp.uint32)
    hi_bits = lax.bitcast_convert_type(hi.astype(BF16).astype(F32), jnp.uint32)
    return (lo_bits >> 16) | (hi_bits & _HI16)


def _unpack_halves(words):
    lo = lax.bitcast_convert_type(words << 16, F32)
    hi = lax.bitcast_convert_type(words & _HI16, F32)
    return lo, hi


def _row_copy(src_hbm, row, dst_ref, r, sem):
    return pltpu.make_async_copy(src_hbm.at[pl.ds(row, 1)], dst_ref.at[pl.ds(r, 1)], sem)


def _wait_rows(src_hbm, dst_ref, sem):
    pltpu.make_async_copy(src_hbm.at[pl.ds(0, dst_ref.shape[0])], dst_ref, sem).wait()


def _gather_kernel(idx_ref, src_hbm, o_ref, sem, *, rows):
    def issue(pair, carry):
        for p in range(2):
            r = 2 * pair + p
            _row_copy(src_hbm, idx_ref[0, r], o_ref, r, sem).start(priority=p)
        return carry

    lax.fori_loop(0, rows // 2, issue, 0)
    _wait_rows(src_hbm, o_ref, sem)


def _gather_rows(src, idx, rows):
    n_t = idx.shape[0] // rows
    w = src.shape[1]
    return pl.pallas_call(
        functools.partial(_gather_kernel, rows=rows),
        grid=(n_t,),
        in_specs=[pl.BlockSpec((None, 1, rows), lambda t: (t, 0, 0), memory_space=pltpu.SMEM),
                  pl.BlockSpec(memory_space=pl.ANY)],
        out_specs=pl.BlockSpec((rows, w), lambda t: (t, 0)),
        out_shape=jax.ShapeDtypeStruct((n_t * rows, w), src.dtype),
        scratch_shapes=[pltpu.SemaphoreType.DMA(())],
        compiler_params=_params(("arbitrary",), 4 * _nbytes((rows, w), src.dtype)),
        name="gather_rows",
    )(idx.reshape(n_t, 1, rows), src)


def _moe_up_kernel(te_ref, x_ref, wg_ref, wu_ref, o_ref, xb_ref):
    half = x_ref.shape[1]

    @pl.when(pl.program_id(1) == 0)
    def _():
        lo, hi = _unpack_halves(x_ref[...])
        xb_ref[:, :half] = lo.astype(BF16)
        xb_ref[:, half:] = hi.astype(BF16)

    x = xb_ref[...]
    g = jnp.dot(x, wg_ref[...], preferred_element_type=F32)
    u = jnp.dot(x, wu_ref[...], preferred_element_type=F32)
    o_ref[...] = (g * jax.nn.sigmoid(g) * u).astype(o_ref.dtype)


def _moe_up(xs, wg, wu, tile_expert):
    n_rows, half = xs.shape
    _, d, f = wg.shape
    rows = _MOE_ROWS
    tn = _divisor_tile(f, 512, _LANE)
    w_spec = pl.BlockSpec((None, d, tn), lambda t, j, te: (te[t], 0, j))
    vmem = 2 * (_nbytes((rows, half), jnp.uint32) + 2 * _nbytes((d, tn), BF16) + _nbytes((rows, tn), BF16)) \
        + _nbytes((rows, d), BF16) + 4 * _nbytes((rows, tn), F32)
    return pl.pallas_call(
        _moe_up_kernel,
        grid_spec=pltpu.PrefetchScalarGridSpec(
            num_scalar_prefetch=1,
            grid=(n_rows // rows, f // tn),
            in_specs=[pl.BlockSpec((rows, half), lambda t, j, te: (t, 0)), w_spec, w_spec],
            out_specs=pl.BlockSpec((rows, tn), lambda t, j, te: (t, j)),
            scratch_shapes=[pltpu.VMEM((rows, d), BF16)]),
        out_shape=jax.ShapeDtypeStruct((n_rows, f), BF16),
        compiler_params=_params(("arbitrary", "arbitrary"), vmem + (4 << 20)),
        name="moe_up",
    )(tile_expert, xs, wg, wu)


def _moe_down_kernel(te_ref, a_ref, wlo_ref, whi_ref, o_ref):
    a = a_ref[...]
    lo = jnp.dot(a, wlo_ref[...], preferred_element_type=F32)
    hi = jnp.dot(a, whi_ref[...], preferred_element_type=F32)
    o_ref[...] = _pack_halves(lo, hi)


def _moe_down(act, wd, tile_expert):
    n_rows, f = act.shape
    d = wd.shape[2]
    half = d // 2
    rows = _MOE_ROWS
    tn = _divisor_tile(half, 512, _LANE)
    nj = half // tn
    vmem = 2 * (_nbytes((rows, f), BF16) + 2 * _nbytes((f, tn), BF16) + _nbytes((rows, tn), jnp.uint32)) \
        + 4 * _nbytes((rows, tn), F32)
    return pl.pallas_call(
        _moe_down_kernel,
        grid_spec=pltpu.PrefetchScalarGridSpec(
            num_scalar_prefetch=1,
            grid=(n_rows // rows, nj),
            in_specs=[pl.BlockSpec((rows, f), lambda t, j, te: (t, 0)),
                      pl.BlockSpec((None, f, tn), lambda t, j, te: (te[t], 0, j)),
                      pl.BlockSpec((None, f, tn), lambda t, j, te: (te[t], 0, nj + j))],
            out_specs=pl.BlockSpec((rows, tn), lambda t, j, te: (t, j))),
        out_shape=jax.ShapeDtypeStruct((n_rows, half), jnp.uint32),
        compiler_params=_params(("arbitrary", "arbitrary"), vmem + (4 << 20)),
        name="moe_down",
    )(tile_expert, act, wd, wd)


def _moe_combine_kernel(i1_ref, i2_ref, y_hbm, r_ref, h_ref, g_ref, b_ref, hf_ref, hb_ref,
                        ya_ref, yb_ref, sem, *, rows, alpha):
    def issue(r, carry):
        _row_copy(y_hbm, i1_ref[0, r], ya_ref, r, sem).start(priority=0)
        _row_copy(y_hbm, i2_ref[0, r], yb_ref, r, sem).start(priority=1)
        return carry

    lax.fori_loop(0, rows, issue, 0)
    _wait_rows(y_hbm, ya_ref, sem)
    _wait_rows(y_hbm, yb_ref, sem)

    half = ya_ref.shape[1]
    w1 = r_ref[:, 0:1]
    w2 = r_ref[:, 1:2]
    a_lo, a_hi = _unpack_halves(ya_ref[...])
    b_lo, b_hi = _unpack_halves(yb_ref[...])
    z_lo = alpha * h_ref[:, :half] + (w1 * a_lo + w2 * b_lo)
    z_hi = alpha * h_ref[:, half:] + (w1 * a_hi + w2 * b_hi)
    inv_d = 1.0 / (2 * half)
    mu = (jnp.sum(z_lo, axis=-1, keepdims=True) + jnp.sum(z_hi, axis=-1, keepdims=True)) * inv_d
    c_lo = z_lo - mu
    c_hi = z_hi - mu
    var = (jnp.sum(c_lo * c_lo, axis=-1, keepdims=True) + jnp.sum(c_hi * c_hi, axis=-1, keepdims=True)) * inv_d
    rs = lax.rsqrt(var + LN_EPS)
    y_lo = c_lo * rs * g_ref[:, :half] + b_ref[:, :half]
    y_hi = c_hi * rs * g_ref[:, half:] + b_ref[:, half:]
    hf_ref[:, :half] = y_lo
    hf_ref[:, half:] = y_hi
    hb_ref[:, :half] = y_lo.astype(BF16)
    hb_ref[:, half:] = y_hi.astype(BF16)


def _moe_combine(ypk, dest1, dest2, top2, h, g, b, alpha):
    m, d = h.shape
    rows = _divisor_tile(m, _COMBINE_ROWS, _SUBLANE)
    n_t = m // rows
    idx_spec = pl.BlockSpec((None, 1, rows), lambda t: (t, 0, 0), memory_space=pltpu.SMEM)
    row = pl.BlockSpec((rows, d), lambda t: (t, 0))
    vec = pl.BlockSpec((1, d), lambda t: (0, 0))
    vmem = 2 * (_nbytes((rows, d), F32) * 2 + _nbytes((rows, d), BF16)) + 2 * _nbytes((rows, d // 2), F32) \
        + 6 * _nbytes((rows, d), F32)
    return pl.pallas_call(
        functools.partial(_moe_combine_kernel, rows=rows, alpha=alpha),
        grid=(n_t,),
        in_specs=[idx_spec, idx_spec, pl.BlockSpec(memory_space=pl.ANY),
                  pl.BlockSpec((rows, _LANE), lambda t: (t, 0)), row, vec, vec],
        out_specs=[row, row],
        out_shape=[jax.ShapeDtypeStruct((m, d), F32), jax.ShapeDtypeStruct((m, d), BF16)],
        scratch_shapes=[pltpu.VMEM((rows, d // 2), jnp.uint32), pltpu.VMEM((rows, d // 2), jnp.uint32),
                        pltpu.SemaphoreType.DMA(())],
        compiler_params=_params(("arbitrary",), vmem),
        name="moe_combine",
    )(dest1.reshape(n_t, 1, rows), dest2.reshape(n_t, 1, rows), ypk, top2, h,
      g.reshape(1, d).astype(F32), b.reshape(1, d).astype(F32))


def _moe_routing(top2, n_exp, rows):
    m = top2.shape[0]
    choice = top2[:, 2:4].astype(jnp.int32).reshape(-1)
    onehot = (choice[:, None] == jnp.arange(n_exp, dtype=jnp.int32)[None, :]).astype(jnp.int32)
    running = jnp.cumsum(onehot, axis=0)
    rank = jnp.sum(onehot * running, axis=1) - 1
    counts = running[-1]
    padded = ((counts + rows - 1) // rows) * rows
    ends = jnp.cumsum(padded)
    starts = ends - padded
    n_tiles = -(-2 * m // rows) + n_exp
    dest = jnp.clip(jnp.sum(onehot * starts[None, :], axis=1) + rank, 0, n_tiles * rows - 1)
    src_tok = jnp.zeros((n_tiles * rows,), jnp.int32).at[dest].set(
        jnp.arange(2 * m, dtype=jnp.int32) // 2)
    tile_start = jnp.arange(n_tiles, dtype=jnp.int32) * rows
    tile_expert = jnp.minimum(jnp.sum((tile_start[:, None] >= ends[None, :]).astype(jnp.int32), axis=1),
                              n_exp - 1)
    return src_tok, tile_expert, dest[0::2], dest[1::2]


def _attn_kernel(slopes_ref, q_ref, k_ref, v_ref, lq1_ref, lk1_ref, lq2_ref, lk2_ref, g_ref,
                 o_ref, qqt_ref, vt_ref, s_ref, m_ref, l_ref, acc_ref, *, tq, tk, qk_dim, scale, lam_init):
    h = pl.program_id(1)
    i = pl.program_id(2)
    slope = slopes_ref[h]
    r = tq // tk
    lp = k_ref.shape[0]
    vdim = v_ref.shape[1]

    @pl.when(i == 0)
    def _():
        def xpose(c, carry):
            c0 = pl.multiple_of(c * tk, tk)
            vt_ref[:, pl.ds(c0, tk)] = v_ref[pl.ds(c0, tk), :].astype(F32).T.astype(BF16)
            return carry
        lax.fori_loop(0, lp // tk, xpose, 0)

    log2e = math.log2(math.e)
    qs = q_ref[...].astype(F32) * (scale * log2e)
    lane = lax.broadcasted_iota(jnp.int32, qs.shape, 1)
    ones = jnp.where(lane < 3, 1.0, 0.0).astype(F32)
    qa1 = jnp.concatenate([jnp.where(lane < qk_dim, qs, 0.0), ones], axis=1)
    qa2 = jnp.concatenate([jnp.where(lane >= qk_dim, qs, 0.0), ones], axis=1)
    qqt_ref[:, :tq] = qa1.T.astype(BF16)
    qqt_ref[:, tq:] = qa2.T.astype(BF16)
    slope2 = slope * log2e
    kj_idx = lax.broadcasted_iota(jnp.int32, (tk, vdim), 0).astype(F32)
    kl_idx = lax.broadcasted_iota(jnp.int32, (tk, vdim), 1)
    bias = slope2 * kj_idx
    b_hi = bias.astype(BF16).astype(F32)
    rem = bias - b_hi
    b_mid = rem.astype(BF16).astype(F32)
    b_lo = rem - b_mid
    kb = jnp.where(kl_idx == 0, b_hi, jnp.where(kl_idx == 1, b_mid, jnp.where(kl_idx == 2, b_lo, 0.0)))
    kb = kb.astype(BF16)

    m_ref[...] = jnp.full(m_ref.shape, _NEG, F32)
    l_ref[...] = jnp.zeros(l_ref.shape, F32)
    acc_ref[...] = jnp.zeros(acc_ref.shape, F32)

    def put_scores(slot, j):
        k0 = pl.multiple_of(j * tk, tk)
        ka = jnp.concatenate([k_ref[pl.ds(k0, tk), :], kb], axis=1)
        for half in range(2):
            s_ref[slot, half] = jnp.dot(ka, qqt_ref[:, half * tq:(half + 1) * tq],
                                        preferred_element_type=F32)

    def absorb(j, slot, masked):
        k0 = pl.multiple_of(j * tk, tk)
        vt = vt_ref[:, pl.ds(k0, tk)]
        off = j * tk - i * tq
        c = slope2 * off.astype(F32)
        for half in range(2):
            s = s_ref[slot, half]
            if masked:
                krow = lax.broadcasted_iota(jnp.int32, s.shape, 0)
                qcol = lax.broadcasted_iota(jnp.int32, s.shape, 1)
                s = jnp.where(krow - qcol <= -off, s, _NEG)
            m_old = m_ref[half]
            m_new = jnp.maximum(m_old, jnp.max(s, axis=0, keepdims=True) + c)
            p = jnp.exp2(s + (c - m_new))
            alpha = jnp.exp2(m_old - m_new)
            l_ref[half] = alpha * l_ref[half] + jnp.sum(p, axis=0, keepdims=True)
            pv = jnp.dot(vt, p.astype(BF16), preferred_element_type=F32)
            acc_ref[half] = alpha * acc_ref[half] + pv
            m_ref[half] = m_new

    n_full = i * r
    odd = n_full & 1

    @pl.when(odd == 1)
    def _():
        put_scores(0, 0)
        absorb(0, 0, False)

    put_scores(0, odd)

    def pair(t, carry):
        j = odd + 2 * t
        put_scores(1, j + 1)
        absorb(j, 0, False)
        put_scores(0, j + 2)
        absorb(j + 1, 1, False)
        return carry

    lax.fori_loop(0, lax.shift_right_logical(n_full - odd, 1), pair, 0)
    for d in range(r):
        if d + 1 < r:
            put_scores((d + 1) % 2, n_full + d + 1)
        absorb(n_full + d, d % 2, True)

    lam = (jnp.exp(jnp.sum(lq1_ref[...] * lk1_ref[...], axis=1, keepdims=True))
           - jnp.exp(jnp.sum(lq2_ref[...] * lk2_ref[...], axis=1, keepdims=True)) + lam_init)
    ot = acc_ref[0] * (1.0 / l_ref[0]) - lam * (acc_ref[1] * (1.0 / l_ref[1]))
    ms = jnp.mean(ot * ot, axis=0, keepdims=True)
    y = ot * lax.rsqrt(ms + LN_EPS) * g_ref[...] * (1.0 - lam_init)
    o_ref[...] = y.T.astype(o_ref.dtype)


def _diff_attention(proj, bsz, lp, d_att, lam_vecs, subln_g, lam_init):
    vdim = d_att // ATT_HEADS
    qk_dim = vdim // 2
    assert vdim == _LANE
    tq = _divisor_tile(lp, _TS_TARGET, _SEQ_ALIGN)
    tk = _divisor_tile(tq, _TK_ATT, _LANE)
    nq = lp // tq
    slopes = jnp.asarray(2.0 ** (-8.0 * np.arange(1, ATT_HEADS + 1) / ATT_HEADS), F32)
    lq1, lk1, lq2, lk2 = [v.reshape(1, qk_dim).astype(F32) for v in lam_vecs]
    vec = pl.BlockSpec((1, qk_dim), lambda b, h, i: (0, 0))
    in_specs = [
        pl.BlockSpec(memory_space=pltpu.SMEM),
        pl.BlockSpec((tq, vdim), lambda b, h, i: (b * nq + i, h)),
        pl.BlockSpec((lp, vdim), lambda b, h, i: (b, ATT_HEADS + h)),
        pl.BlockSpec((lp, vdim), lambda b, h, i: (b, 2 * ATT_HEADS + h)),
        vec, vec, vec, vec,
        pl.BlockSpec((vdim, 1), lambda b, h, i: (0, 0)),
    ]
    vmem = 2 * (2 * _nbytes((lp, vdim), BF16) + 2 * _nbytes((tq, vdim), BF16)) \
        + _nbytes((lp, vdim), BF16) + 4 * _nbytes((2 * tq, vdim), F32) + 6 * _nbytes((2 * tq, tk), F32)
    return pl.pallas_call(
        functools.partial(_attn_kernel, tq=tq, tk=tk, qk_dim=qk_dim, scale=qk_dim ** -0.5,
                          lam_init=lam_init),
        grid=(bsz, ATT_HEADS, nq),
        in_specs=in_specs,
        out_specs=pl.BlockSpec((tq, vdim), lambda b, h, i: (b * nq + i, h)),
        out_shape=jax.ShapeDtypeStruct((bsz * lp, d_att), BF16),
        scratch_shapes=[pltpu.VMEM((2 * vdim, 2 * tq), BF16), pltpu.VMEM((vdim, lp), BF16),
                        pltpu.VMEM((2, 2, tk, tq), F32),
                        pltpu.VMEM((2, 1, tq), F32), pltpu.VMEM((2, 1, tq), F32),
                        pltpu.VMEM((2, vdim, tq), F32)],
        compiler_params=_params(("parallel", "parallel", "arbitrary"), vmem + (8 << 20)),
        name="diff_attention",
    )(slopes, proj, proj, proj, lq1, lk1, lq2, lk2, subln_g.reshape(vdim, 1).astype(F32))


_CONV_HALO = 32
_CONV_ROWS = 32


def _conv_kernel(a_ref, gate_ref, dw_ref, dwb_ref, lng_ref, lnb_ref, pw_ref, pwb_ref, o_ref,
                 hbuf_ref, cbuf_ref, sh_ref, *, ts, k_len):
    t = pl.program_id(1)
    c_dim = a_ref.shape[1]

    @pl.when(t == 0)
    def _():
        hbuf_ref[0:_CONV_HALO, :] = jnp.zeros((_CONV_HALO, c_dim), F32)

    @pl.when(t > 0)
    def _():
        hbuf_ref[0:_CONV_HALO, :] = hbuf_ref[ts:ts + _CONV_HALO, :]

    a = a_ref[...].astype(F32)
    hbuf_ref[_CONV_HALO:_CONV_HALO + ts, :] = a * jax.nn.sigmoid(gate_ref[...].astype(F32))

    first = _CONV_HALO - (k_len - 1)
    sh_rows = sh_ref.shape[1]
    for lb in range(c_dim // _LANE):
        lanes = slice(lb * _LANE, (lb + 1) * _LANE)
        for b in range(1, _SUBLANE):
            sh_ref[b - 1] = hbuf_ref[b:b + sh_rows, lanes]
        taps = [dw_ref[k:k + 1, lanes] for k in range(k_len)]
        bias = dwb_ref[:, lanes]

        def chunk(c, carry, lanes=lanes, taps=taps, bias=bias):
            base = c * _CONV_ROWS
            acc = jnp.broadcast_to(bias, (_CONV_ROWS, _LANE))
            for k in range(k_len):
                a, b = divmod(first + k, _SUBLANE)
                start = pl.multiple_of(base + a * _SUBLANE, _SUBLANE)
                if b == 0:
                    src = hbuf_ref[pl.ds(start, _CONV_ROWS), lanes]
                else:
                    src = sh_ref[b - 1, pl.ds(start, _CONV_ROWS), :]
                acc = acc + taps[k] * src
            cbuf_ref[pl.ds(pl.multiple_of(base, _CONV_ROWS), _CONV_ROWS), lanes] = acc
            return carry

        lax.fori_loop(0, ts // _CONV_ROWS, chunk, 0)

    y = cbuf_ref[...]
    mu = jnp.mean(y, axis=-1, keepdims=True)
    yc = y - mu
    var = jnp.mean(yc * yc, axis=-1, keepdims=True)
    yn = yc * lax.rsqrt(var + LN_EPS) * lng_ref[...] + lnb_ref[...]
    act = (yn * jax.nn.sigmoid(yn)).astype(BF16)
    out = jnp.dot(act, pw_ref[...], preferred_element_type=F32) + pwb_ref[...]
    o_ref[...] = out.astype(o_ref.dtype)


def _conformer_conv(proj, bsz, lp, col0, d_conv, dw_w, dw_b, ln_g, ln_b, pw_w, pw_b):
    ts = _divisor_tile(lp, _TS_TARGET, _CONV_ROWS)
    nt = lp // ts
    cb = col0 // d_conv
    assert col0 % d_conv == 0 and dw_w.shape[0] - 1 <= _CONV_HALO
    vec = pl.BlockSpec((1, d_conv), lambda b, t: (0, 0))
    in_specs = [
        pl.BlockSpec((ts, d_conv), lambda b, t: (b * nt + t, cb)),
        pl.BlockSpec((ts, d_conv), lambda b, t: (b * nt + t, cb + 1)),
        pl.BlockSpec((dw_w.shape[0], d_conv), lambda b, t: (0, 0)),
        vec, vec, vec,
        pl.BlockSpec((d_conv, d_conv), lambda b, t: (0, 0)),
        vec,
    ]
    vmem = 4 * _nbytes((ts, d_conv), BF16) * 2 + 8 * _nbytes((ts + _CONV_HALO, d_conv), F32)
    r1 = lambda v: v.reshape(1, d_conv).astype(F32)
    return pl.pallas_call(
        functools.partial(_conv_kernel, ts=ts, k_len=dw_w.shape[0]),
        grid=(bsz, nt),
        in_specs=in_specs,
        out_specs=pl.BlockSpec((ts, d_conv), lambda b, t: (b * nt + t, 0)),
        out_shape=jax.ShapeDtypeStruct((bsz * lp, d_conv), BF16),
        scratch_shapes=[pltpu.VMEM((ts + _CONV_HALO, d_conv), F32), pltpu.VMEM((ts, d_conv), F32),
                        pltpu.VMEM((_SUBLANE - 1, ts + _CONV_HALO - _SUBLANE, _LANE), F32)],
        compiler_params=_params(("parallel", "arbitrary"), vmem),
        name="conformer_conv",
    )(proj, proj, dw_w.astype(F32), r1(dw_b), r1(ln_g), r1(ln_b), pw_w.astype(BF16), r1(pw_b))


_POOL_HALO = 16


def _pool_kernel(u_ref, w_ref, b_ref, s_ref, o_ref, buf_ref, *, ts, windows, gw):
    t = pl.program_id(1)
    c_dim = u_ref.shape[1]

    @pl.when(t == 0)
    def _():
        buf_ref[0:_POOL_HALO, :] = jnp.zeros((_POOL_HALO, c_dim), F32)

    @pl.when(t > 0)
    def _():
        buf_ref[0:_POOL_HALO, :] = buf_ref[ts:ts + _POOL_HALO, :]

    buf_ref[_POOL_HALO:_POOL_HALO + ts, :] = u_ref[...].astype(F32)
    pos = (t * ts + 1 + lax.broadcasted_iota(jnp.int32, (ts, 1), 0)).astype(F32)
    for gi, win in enumerate(windows):
        lanes = slice(gi * gw, (gi + 1) * gw)
        u = buf_ref[_POOL_HALO:_POOL_HALO + ts, lanes]
        tot = u
        for j in range(1, win):
            tot = tot + buf_ref[_POOL_HALO - j:_POOL_HALO - j + ts, lanes]
        d = tot / jnp.minimum(pos, float(win)) - u
        y = jnp.dot(d.astype(BF16), w_ref[gi], preferred_element_type=F32)
        o_ref[:, lanes] = ((y + b_ref[:, lanes]) * s_ref[:, lanes]).astype(o_ref.dtype)


def _multiscale_pool(proj, bsz, lp, col0, d_pool, w, bias, scale):
    n_g = len(POOL_WINDOWS)
    gw = d_pool // n_g
    ts = _divisor_tile(lp, _TS_TARGET, _SUBLANE)
    nt = lp // ts
    assert col0 % d_pool == 0 and max(POOL_WINDOWS) - 1 <= _POOL_HALO
    cb = col0 // d_pool
    vec = pl.BlockSpec((1, d_pool), lambda b, t: (0, 0))
    r1 = lambda v: v.reshape(1, d_pool).astype(F32)
    return pl.pallas_call(
        functools.partial(_pool_kernel, ts=ts, windows=POOL_WINDOWS, gw=gw),
        grid=(bsz, nt),
        in_specs=[pl.BlockSpec((ts, d_pool), lambda b, t: (b * nt + t, cb)),
                  pl.BlockSpec((n_g, gw, gw), lambda b, t: (0, 0, 0)), vec, vec],
        out_specs=pl.BlockSpec((ts, d_pool), lambda b, t: (b * nt + t, 0)),
        out_shape=jax.ShapeDtypeStruct((bsz * lp, d_pool), BF16),
        scratch_shapes=[pltpu.VMEM((ts + _POOL_HALO, d_pool), F32)],
        compiler_params=_params(("parallel", "arbitrary"), 10 * _nbytes((ts, d_pool), F32)),
        name="multiscale_pool",
    )(proj, w.astype(BF16), r1(bias), r1(scale))


_SSD_HALO = 8


def _bf16_terms(x):
    hi = x.astype(BF16)
    rem = x - hi.astype(F32)
    mid = rem.astype(BF16)
    lo = (rem - mid.astype(F32)).astype(BF16)
    return hi, mid, lo


def _ssd_kernel(z_ref, x_ref, bc_ref, dt_ref, cw_ref, cb_ref, dtb_ref, alog_ref, dskip_ref, ng_ref,
                expand_ref, o_ref, ext_ref, state_ref, *, tc, d_ssm, k_len):
    c = pl.program_id(1)
    n_state = SSM_STATE
    gwid = d_ssm // SSM_GROUPS
    hpg = gwid // SSM_HEADDIM
    xbc_dim = x_ref.shape[1] + bc_ref.shape[1]

    @pl.when(c == 0)
    def _():
        ext_ref[0:_SSD_HALO, :] = jnp.zeros((_SSD_HALO, xbc_dim), F32)
        state_ref[...] = jnp.zeros(state_ref.shape, F32)

    @pl.when(c > 0)
    def _():
        ext_ref[0:_SSD_HALO, :] = ext_ref[tc:tc + _SSD_HALO, :]

    ext_ref[_SSD_HALO:_SSD_HALO + tc, :d_ssm] = x_ref[...].astype(F32)
    ext_ref[_SSD_HALO:_SSD_HALO + tc, d_ssm:] = bc_ref[...].astype(F32)

    first = _SSD_HALO - (k_len - 1)
    pre = jnp.broadcast_to(cb_ref[...], (tc, xbc_dim))
    for k in range(k_len):
        pre = pre + cw_ref[k:k + 1, :] * ext_ref[first + k:first + k + tc, :]
    xbc = pre * jax.nn.sigmoid(pre)
    xs = xbc[:, :d_ssm]
    bm = xbc[:, d_ssm:d_ssm + SSM_GROUPS * n_state].astype(BF16)
    cm = xbc[:, d_ssm + SSM_GROUPS * n_state:].astype(BF16)

    dt = jax.nn.softplus(dt_ref[...] + dtb_ref[...])
    a = dt * (-jnp.exp(alog_ref[...]))
    ti = lax.broadcasted_iota(jnp.int32, (tc, tc), 0)
    si = lax.broadcasted_iota(jnp.int32, (tc, tc), 1)
    causal = ti >= si
    tri = jnp.where(causal, 1.0, 0.0).astype(BF16)
    a_cs = sum(jnp.dot(tri, part, preferred_element_type=F32) for part in _bf16_terms(a))
    a_cs_t = a_cs.T
    expand = expand_ref[...]
    a_cs_x = sum(jnp.dot(part, expand, preferred_element_type=F32) for part in _bf16_terms(a_cs))
    dt_x = sum(jnp.dot(part, expand, preferred_element_type=F32) for part in _bf16_terms(dt))
    a_last = a_cs_x[tc - 1:tc, :]
    x_dt = xs * dt_x
    x_end = (x_dt * jnp.exp(a_last - a_cs_x)).astype(BF16)
    x_dt_b = x_dt.astype(BF16)
    decay_in = jnp.exp(a_cs_x)
    decay_chunk = jnp.exp(a_last)

    lane_g = lax.broadcasted_iota(jnp.int32, (tc, gwid), 1)
    ys = []
    for g in range(SSM_GROUPS):
        cg = cm[:, g * n_state:(g + 1) * n_state]
        bg = bm[:, g * n_state:(g + 1) * n_state]
        gl = slice(g * gwid, (g + 1) * gwid)
        cb = lax.dot_general(cg, bg, (((1,), (1,)), ((), ())), preferred_element_type=F32)
        xg = x_dt_b[:, gl]
        y_g = None
        for r in range(hpg):
            hd = g * hpg + r
            seg = a_cs[:, hd:hd + 1] - a_cs_t[hd:hd + 1, :]
            lmat = jnp.exp(jnp.where(causal, seg, _NEG))
            mh = (cb * lmat).astype(BF16)
            in_head = (lane_g >= r * SSM_HEADDIM) & (lane_g < (r + 1) * SSM_HEADDIM)
            xh = jnp.where(in_head, xg, jnp.zeros_like(xg))
            yh = jnp.dot(mh, xh, preferred_element_type=F32)
            y_g = yh if y_g is None else y_g + yh
        st = state_ref[g]
        y_g = y_g + jnp.dot(cg, st.astype(BF16), preferred_element_type=F32) * decay_in[:, gl]
        upd = lax.dot_general(bg, x_end[:, gl], (((0,), (0,)), ((), ())), preferred_element_type=F32)
        state_ref[g] = decay_chunk[:, gl] * st + upd
        ys.append(y_g)
    y = jnp.concatenate(ys, axis=1) + dskip_ref[...] * xs
    zf = z_ref[...].astype(F32)
    y = y * (zf * jax.nn.sigmoid(zf))
    outs = []
    for g in range(SSM_GROUPS):
        yg = y[:, g * gwid:(g + 1) * gwid]
        ms = jnp.mean(yg * yg, axis=-1, keepdims=True)
        outs.append(yg * lax.rsqrt(ms + LN_EPS))
    o_ref[...] = (jnp.concatenate(outs, axis=1) * ng_ref[...]).astype(o_ref.dtype)


def _mamba2_ssd(proj, dt_raw, bsz, lp, z_col0, d_ssm, conv_w, conv_b, dt_bias, a_log, d_skip, norm_g):
    xbc_dim = d_ssm + 2 * SSM_GROUPS * SSM_STATE
    n_heads = d_ssm // SSM_HEADDIM
    tc = _SSD_CHUNK
    nc = lp // tc
    bc_dim = xbc_dim - d_ssm
    assert lp % tc == 0 and z_col0 % d_ssm == 0 and (z_col0 + 2 * d_ssm) % bc_dim == 0
    zb = z_col0 // d_ssm
    bcb = (z_col0 + 2 * d_ssm) // bc_dim
    pad_h = lambda v: jnp.pad(v.astype(F32).reshape(1, n_heads), ((0, 0), (0, _LANE - n_heads)))
    expand = (np.arange(d_ssm)[None, :] // SSM_HEADDIM == np.arange(_LANE)[:, None]).astype(np.float32)
    d_x = jnp.repeat(d_skip.astype(F32), SSM_HEADDIM).reshape(1, d_ssm)
    full = lambda shape: pl.BlockSpec(shape, lambda b, c: (0,) * len(shape))
    in_specs = [
        pl.BlockSpec((tc, d_ssm), lambda b, c: (b * nc + c, zb)),
        pl.BlockSpec((tc, d_ssm), lambda b, c: (b * nc + c, zb + 1)),
        pl.BlockSpec((tc, bc_dim), lambda b, c: (b * nc + c, bcb)),
        pl.BlockSpec((tc, _LANE), lambda b, c: (b * nc + c, 0)),
        full((conv_w.shape[0], xbc_dim)), full((1, xbc_dim)),
        full((1, _LANE)), full((1, _LANE)), full((1, d_ssm)), full((1, d_ssm)),
        full((_LANE, d_ssm)),
    ]
    return pl.pallas_call(
        functools.partial(_ssd_kernel, tc=tc, d_ssm=d_ssm, k_len=conv_w.shape[0]),
        grid=(bsz, nc),
        in_specs=in_specs,
        out_specs=pl.BlockSpec((tc, d_ssm), lambda b, c: (b * nc + c, 0)),
        out_shape=jax.ShapeDtypeStruct((bsz * lp, d_ssm), BF16),
        scratch_shapes=[pltpu.VMEM((tc + _SSD_HALO, xbc_dim), F32),
                        pltpu.VMEM((SSM_GROUPS, SSM_STATE, d_ssm // SSM_GROUPS), F32)],
        compiler_params=_params(("parallel", "arbitrary"), 32 << 20),
        name="mamba2_ssd",
    )(proj, proj, proj, dt_raw, conv_w.astype(F32), conv_b.reshape(1, xbc_dim).astype(F32),
      pad_h(dt_bias), pad_h(a_log), d_x, norm_g.reshape(1, d_ssm).astype(F32),
      jnp.asarray(expand, dtype=BF16))


def kernel(x, meta, emb_ln_g, emb_ln_b, w_in, lam_q1, lam_k1, lam_q2, lam_k2, att_subln_g, cf_dw_w, cf_dw_b, cf_ln_g, cf_ln_b, cf_pw_w, cf_pw_b, pool_w, pool_b, pool_scale, ssm_conv_w, ssm_conv_b, ssm_dt_bias, ssm_a_log, ssm_d, ssm_norm_g, w_out, ln1_g, ln1_b, ln2_g, ln2_b, ffn_w_gate, ffn_w_up, ffn_w_down, moe_router, moe_w_gate, moe_w_up, moe_w_down):
    bsz, seq, d_model = x.shape
    depth = w_in.shape[0]
    n_meta = meta.shape[0]
    d_att = d_conv = d_pool = d_ssm = d_model // 4
    xbc_dim = d_ssm + 2 * SSM_GROUPS * SSM_STATE
    d_main = 3 * d_att + 2 * d_conv + d_pool + d_ssm + xbc_dim
    alpha = (2 * depth) ** 0.25

    seq_len = n_meta + seq
    lp = -(-seq_len // _SEQ_ALIGN) * _SEQ_ALIGN
    h, h16 = _embed_layer_norm(x, meta, emb_ln_g, emb_ln_b, lp)

    for l in range(depth):
        proj = _matmul([h16], _weight_bf16(w_in, l, 1, d_main)[0], BF16)
        dt_raw = _matmul([h16], _weight_tail_bf16(w_in, l, d_main), F32)
        lam_init = 0.8 - 0.6 * math.exp(-0.3 * l)

        y_att = _diff_attention(proj, bsz, lp, d_att, (lam_q1[l], lam_k1[l], lam_q2[l], lam_k2[l]),
                                att_subln_g[l], lam_init)
        y_conv = _conformer_conv(proj, bsz, lp, 3 * d_att, d_conv, cf_dw_w[l], cf_dw_b[l],
                                 cf_ln_g[l], cf_ln_b[l], cf_pw_w[l], cf_pw_b[l])
        y_pool = _multiscale_pool(proj, bsz, lp, 3 * d_att + 2 * d_conv, d_pool,
                                  pool_w[l], pool_b[l], pool_scale[l])
        y_ssm = _mamba2_ssd(proj, dt_raw, bsz, lp, 3 * d_att + 2 * d_conv + d_pool, d_ssm,
                            ssm_conv_w[l], ssm_conv_b[l], ssm_dt_bias[l], ssm_a_log[l], ssm_d[l],
                            ssm_norm_g[l])
        z = _matmul([y_att, y_conv, y_pool, y_ssm], _weight_bf16(w_out, l, 1)[0], F32,
                    residual=h, alpha=alpha)

        e = l // 2
        if l % 2 == 0:
            h, h16 = _layer_norm(z, ln1_g[l], ln1_b[l])
            act = _swiglu_up(h16, _weight_bf16(ffn_w_gate, e, 1)[0], _weight_bf16(ffn_w_up, e, 1)[0])
            z = _matmul([act], _weight_bf16(ffn_w_down, e, 1)[0], F32, residual=h, alpha=alpha)
            h, h16 = _layer_norm(z, ln2_g[l], ln2_b[l])
        else:
            n_exp = moe_router.shape[2]
            expert_bf16 = lambda w: _weight_bf16(w.reshape((-1,) + w.shape[2:]), e * n_exp, n_exp)
            h, h16, h_packed = _layer_norm(z, ln1_g[l], ln1_b[l], packed=True)
            top2 = _router_top2(h, moe_router[e])
            src_tok, tile_expert, dest1, dest2 = _moe_routing(top2, n_exp, _MOE_ROWS)
            xs = _gather_rows(h_packed, src_tok, _MOE_ROWS)
            act = _moe_up(xs, expert_bf16(moe_w_gate), expert_bf16(moe_w_up), tile_expert)
            y_sorted = _moe_down(act, expert_bf16(moe_w_down), tile_expert)
            h, h16 = _moe_combine(y_sorted, dest1, dest2, top2, h, ln2_g[l], ln2_b[l], alpha)

    return h.reshape(bsz, lp, d_model)[:, n_meta:seq_len]
```

```python
import functools
import math

import numpy as np
import jax
import jax.numpy as jnp
from jax import lax
from jax.experimental import pallas as pl
from jax.experimental.pallas import tpu as pltpu

F32 = jnp.float32
BF16 = jnp.bfloat16
HIGHEST = lax.Precision.HIGHEST

N_META = 16
ATT_HEADS = 8
CONV_K = 31
POOL_WINDOWS = (2, 4, 8, 16)
SSM_HEADDIM = 64
SSM_GROUPS = 4
SSM_STATE = 128
SSM_CONV_K = 4
N_EXPERTS = 8
LN_EPS = 1e-5

_LANE = 128
_SUBLANE = 8
_VMEM_CAP = 60 * 1024 * 1024

_SEQ_ALIGN = 256
_TM_TARGET = 1536
_TS_TARGET = 768
_TK_ATT = 256
_SSD_CHUNK = 128
_LN_ROWS = 256
_NEG = -1e30


def _divisor_tile(n, target, align):
    best = align
    t = align
    while t <= min(n, target):
        if n % t == 0:
            best = t
        t += align
    assert n % best == 0, (n, target, align)
    return best


def _params(sem, vmem_bytes):
    return pltpu.CompilerParams(
        dimension_semantics=sem,
        vmem_limit_bytes=int(min(max(vmem_bytes, 32 * 1024 * 1024), _VMEM_CAP)))


def _nbytes(shape, dtype):
    return int(np.prod(shape)) * jnp.dtype(dtype).itemsize


def _ln_kernel(z_ref, g_ref, b_ref, hf_ref, hb_ref, *maybe_packed_ref):
    z = z_ref[...]
    mu = jnp.mean(z, axis=-1, keepdims=True)
    zc = z - mu
    var = jnp.mean(zc * zc, axis=-1, keepdims=True)
    y = zc * lax.rsqrt(var + LN_EPS) * g_ref[...] + b_ref[...]
    hf_ref[...] = y
    hb_ref[...] = y.astype(BF16)
    if maybe_packed_ref:
        half = y.shape[1] // 2
        maybe_packed_ref[0][...] = _pack_halves(y[:, :half], y[:, half:])


def _layer_norm(z, g, b, packed=False):
    m, d = z.shape
    tm = _divisor_tile(m, _LN_ROWS, _SUBLANE)
    row = pl.BlockSpec((tm, d), lambda i: (i, 0))
    vec = pl.BlockSpec((1, d), lambda i: (0, 0))
    out_specs = [row, row]
    out_shape = [jax.ShapeDtypeStruct((m, d), F32), jax.ShapeDtypeStruct((m, d), BF16)]
    if packed:
        out_specs.append(pl.BlockSpec((tm, d // 2), lambda i: (i, 0)))
        out_shape.append(jax.ShapeDtypeStruct((m, d // 2), jnp.uint32))
    return pl.pallas_call(
        _ln_kernel,
        grid=(m // tm,),
        in_specs=[row, vec, vec],
        out_specs=out_specs,
        out_shape=out_shape,
        compiler_params=_params(("parallel",), 10 * _nbytes((tm, d), F32)),
        name="layer_norm",
    )(z, g.reshape(1, d).astype(F32), b.reshape(1, d).astype(F32))


def _embed_ln_kernel(prev_ref, cur_ref, meta_ref, g_ref, b_ref, hf_ref, hb_ref, *, rows, seq_len):
    t = pl.program_id(1)
    n_meta = meta_ref.shape[0]
    top = jnp.where(t == 0, meta_ref[...], prev_ref[rows - n_meta:, :])
    tok = jnp.concatenate([top, cur_ref[:rows - n_meta, :]], axis=0)
    pos = t * rows + lax.broadcasted_iota(jnp.int32, (rows, 1), 0)
    z = jnp.where(pos < seq_len, tok, 0.0)
    mu = jnp.mean(z, axis=-1, keepdims=True)
    zc = z - mu
    var = jnp.mean(zc * zc, axis=-1, keepdims=True)
    y = zc * lax.rsqrt(var + LN_EPS) * g_ref[...] + b_ref[...]
    hf_ref[...] = y
    hb_ref[...] = y.astype(BF16)


def _embed_layer_norm(x, meta, g, b, lp):
    bsz, seq, d = x.shape
    n_meta = meta.shape[0]
    rows = _divisor_tile(math.gcd(seq, lp), _LN_ROWS, _SUBLANE)
    assert n_meta % _SUBLANE == 0 and n_meta <= rows
    nx = seq // rows
    nt = lp // rows
    out_row = pl.BlockSpec((rows, d), lambda bi, t: (bi * nt + t, 0))
    vec = pl.BlockSpec((1, d), lambda bi, t: (0, 0))
    return pl.pallas_call(
        functools.partial(_embed_ln_kernel, rows=rows, seq_len=n_meta + seq),
        grid=(bsz, nt),
        in_specs=[pl.BlockSpec((None, rows, d), lambda bi, t: (bi, jnp.clip(t - 1, 0, nx - 1), 0)),
                  pl.BlockSpec((None, rows, d), lambda bi, t: (bi, jnp.minimum(t, nx - 1), 0)),
                  pl.BlockSpec((n_meta, d), lambda bi, t: (0, 0)), vec, vec],
        out_specs=[out_row, out_row],
        out_shape=[jax.ShapeDtypeStruct((bsz * lp, d), F32), jax.ShapeDtypeStruct((bsz * lp, d), BF16)],
        compiler_params=_params(("parallel", "parallel"), 12 * _nbytes((rows, d), F32)),
        name="embed_layer_norm",
    )(x, x, meta.astype(F32), g.reshape(1, d).astype(F32), b.reshape(1, d).astype(F32))


def _cast_kernel(w_ref, o_ref):
    o_ref[...] = w_ref[...].astype(o_ref.dtype)


def _weight_bf16(w, s0, n_s, n_cols=None):
    _, r, c = w.shape
    n_cols = c if n_cols is None else n_cols
    rb = _divisor_tile(r, 512, _SUBLANE)
    cb = _divisor_tile(n_cols, 2048, _LANE)
    return pl.pallas_call(
        _cast_kernel,
        grid=(n_s, r // rb, n_cols // cb),
        in_specs=[pl.BlockSpec((None, rb, cb), lambda s, i, j: (s0 + s, i, j))],
        out_specs=pl.BlockSpec((None, rb, cb), lambda s, i, j: (s, i, j)),
        out_shape=jax.ShapeDtypeStruct((n_s, r, n_cols), BF16),
        compiler_params=_params(("parallel", "parallel", "parallel"), 4 * _nbytes((rb, cb), F32)),
        name="weight_bf16",
    )(w)


def _cast_t_kernel(w_ref, o_ref):
    o_ref[...] = w_ref[...].T.astype(o_ref.dtype)


def _weight_t_bf16(w_t, s, n_cols):
    _, c, r = w_t.shape
    rb = _divisor_tile(r, 512, _LANE)
    cb = _divisor_tile(n_cols, 512, _LANE)
    return pl.pallas_call(
        _cast_t_kernel,
        grid=(r // rb, n_cols // cb),
        in_specs=[pl.BlockSpec((None, cb, rb), lambda i, j: (s, j, i))],
        out_specs=pl.BlockSpec((rb, cb), lambda i, j: (i, j)),
        out_shape=jax.ShapeDtypeStruct((r, n_cols), BF16),
        compiler_params=_params(("parallel", "parallel"), 32 << 20),
        name="weight_t_bf16",
    )(w_t)


def _tail_cast_t_kernel(w_ref, o_ref):
    w = w_ref[...]
    pad = jnp.zeros((_LANE - w.shape[0], w.shape[1]), w.dtype)
    o_ref[...] = jnp.concatenate([w, pad], axis=0).T.astype(o_ref.dtype)


def _weight_t_tail_bf16(w_t, s, col0):
    _, c, r = w_t.shape
    n_tail = c - col0
    assert n_tail % _SUBLANE == 0 and col0 % n_tail == 0 and n_tail <= _LANE
    rb = _divisor_tile(r, 512, _LANE)
    return pl.pallas_call(
        _tail_cast_t_kernel,
        grid=(r // rb,),
        in_specs=[pl.BlockSpec((None, n_tail, rb), lambda i: (s, col0 // n_tail, i))],
        out_specs=pl.BlockSpec((rb, _LANE), lambda i: (i, 0)),
        out_shape=jax.ShapeDtypeStruct((r, _LANE), BF16),
        compiler_params=_params(("parallel",), 32 << 20),
        name="weight_t_tail_bf16",
    )(w_t)


def _mm_kernel(*refs, n_in, alpha, has_res):
    x_refs = refs[:n_in]
    w_refs = refs[n_in:2 * n_in]
    o_ref = refs[-1]
    acc = None
    for x_ref, w_ref in zip(x_refs, w_refs):
        d = jnp.dot(x_ref[...], w_ref[...], preferred_element_type=F32)
        acc = d if acc is None else acc + d
    if has_res:
        acc = acc + alpha * refs[2 * n_in][...]
    o_ref[...] = acc.astype(o_ref.dtype)


def _matmul(xs, w, out_dtype, residual=None, alpha=1.0):
    n_in = len(xs)
    m, kp = xs[0].shape
    k_total, n = w.shape
    assert k_total == n_in * kp
    tm = _divisor_tile(m, _TM_TARGET if k_total <= 4096 else _TM_TARGET // 2, _LANE)
    tn = _divisor_tile(n, 512 if k_total <= 4096 else 256, _LANE)
    if k_total > 8192:
        tm = _divisor_tile(m, _TM_TARGET // 4, _LANE)
    in_specs = [pl.BlockSpec((tm, kp), lambda i, j: (i, 0)) for _ in range(n_in)]
    in_specs += [pl.BlockSpec((kp, tn), lambda i, j, p=p: (p, j)) for p in range(n_in)]
    args = list(xs) + [w] * n_in
    vmem = 2 * (n_in * _nbytes((tm, kp), BF16) + n_in * _nbytes((kp, tn), BF16)
                + _nbytes((tm, tn), out_dtype)) + 2 * _nbytes((tm, tn), F32)
    if residual is not None:
        in_specs.append(pl.BlockSpec((tm, tn), lambda i, j: (i, j)))
        args.append(residual)
        vmem += 2 * _nbytes((tm, tn), F32)
    return pl.pallas_call(
        functools.partial(_mm_kernel, n_in=n_in, alpha=alpha, has_res=residual is not None),
        grid=(m // tm, n // tn),
        in_specs=in_specs,
        out_specs=pl.BlockSpec((tm, tn), lambda i, j: (i, j)),
        out_shape=jax.ShapeDtypeStruct((m, n), out_dtype),
        compiler_params=_params(("parallel", "arbitrary"), vmem + (4 << 20)),
        name="matmul",
    )(*args)


def _glu_kernel(x_ref, wg_ref, wu_ref, o_ref):
    x = x_ref[...]
    g = jnp.dot(x, wg_ref[...], preferred_element_type=F32)
    u = jnp.dot(x, wu_ref[...], preferred_element_type=F32)
    o_ref[...] = (g * jax.nn.sigmoid(g) * u).astype(o_ref.dtype)


def _swiglu_up(x, wg, wu):
    m, d = x.shape
    f = wg.shape[1]
    tm = _divisor_tile(m, _TM_TARGET, _LANE)
    tn = _divisor_tile(f, 256, _LANE)
    w_spec = pl.BlockSpec((d, tn), lambda i, j: (0, j))
    vmem = 2 * (_nbytes((tm, d), BF16) + 2 * _nbytes((d, tn), BF16) + _nbytes((tm, tn), BF16)) \
        + 4 * _nbytes((tm, tn), F32)
    return pl.pallas_call(
        _glu_kernel,
        grid=(m // tm, f // tn),
        in_specs=[pl.BlockSpec((tm, d), lambda i, j: (i, 0)), w_spec, w_spec],
        out_specs=pl.BlockSpec((tm, tn), lambda i, j: (i, j)),
        out_shape=jax.ShapeDtypeStruct((m, f), BF16),
        compiler_params=_params(("parallel", "arbitrary"), vmem + (4 << 20)),
        name="swiglu_up",
    )(x, wg, wu)


def _router_kernel(h_ref, r_ref, o_ref, *, n_exp):
    logits = jnp.dot(h_ref[...], r_ref[...], preferred_element_type=F32, precision=HIGHEST)
    lane = lax.broadcasted_iota(jnp.int32, logits.shape, 1).astype(F32)
    lg = jnp.where(lane < n_exp, logits, _NEG)
    m1 = jnp.max(lg, axis=1, keepdims=True)
    i1 = jnp.min(jnp.where(lg == m1, lane, float(_LANE)), axis=1, keepdims=True)
    lg2 = jnp.where(lane == i1, _NEG, lg)
    m2 = jnp.max(lg2, axis=1, keepdims=True)
    i2 = jnp.min(jnp.where(lg2 == m2, lane, float(_LANE)), axis=1, keepdims=True)
    e = jnp.exp(m2 - m1)
    w1 = 1.0 / (1.0 + e)
    w2 = e / (1.0 + e)
    o_ref[...] = (jnp.where(lane == 0.0, w1, 0.0) + jnp.where(lane == 1.0, w2, 0.0)
                  + jnp.where(lane == 2.0, i1, 0.0) + jnp.where(lane == 3.0, i2, 0.0))


def _router_top2(h, router):
    m, d = h.shape
    n_exp = router.shape[1]
    tm = _divisor_tile(m, 512, _SUBLANE)
    r_pad = jnp.pad(router.astype(F32), ((0, 0), (0, _LANE - n_exp)))
    return pl.pallas_call(
        functools.partial(_router_kernel, n_exp=n_exp),
        grid=(m // tm,),
        in_specs=[pl.BlockSpec((tm, d), lambda i: (i, 0)), pl.BlockSpec((d, _LANE), lambda i: (0, 0))],
        out_specs=pl.BlockSpec((tm, _LANE), lambda i: (i, 0)),
        out_shape=jax.ShapeDtypeStruct((m, _LANE), F32),
        compiler_params=_params(("parallel",), 4 * _nbytes((tm, d), F32) + (8 << 20)),
        name="router",
    )(h, r_pad)


_HI16 = np.uint32(0xFFFF0000)
_MOE_ROWS = 512
_COMBINE_ROWS = 256


def _pack_halves(lo, hi):
    lo_bits = lax.bitcast_convert_type(lo.astype(BF16).astype(F32), jnp.uint32)
    hi_bits = lax.bitcast_convert_type(hi.astype(BF16).astype(F32), jnp.uint32)
    return (lo_bits >> 16) | (hi_bits & _HI16)


def _unpack_halves(words):
    lo = lax.bitcast_convert_type(words << 16, F32)
    hi = lax.bitcast_convert_type(words & _HI16, F32)
    return lo, hi


def _row_copy(src_hbm, row, dst_ref, r, sem):
    return pltpu.make_async_copy(src_hbm.at[pl.ds(row, 1)], dst_ref.at[pl.ds(r, 1)], sem)


def _wait_rows(src_hbm, dst_ref, sem):
    pltpu.make_async_copy(src_hbm.at[pl.ds(0, dst_ref.shape[0])], dst_ref, sem).wait()


def _gather_kernel(idx_ref, src_hbm, o_ref, sem, *, rows):
    def issue(pair, carry):
        for p in range(2):
            r = 2 * pair + p
            _row_copy(src_hbm, idx_ref[0, r], o_ref, r, sem).start(priority=p)
        return carry

    lax.fori_loop(0, rows // 2, issue, 0)
    _wait_rows(src_hbm, o_ref, sem)


def _gather_rows(src, idx, rows):
    n_t = idx.shape[0] // rows
    w = src.shape[1]
    return pl.pallas_call(
        functools.partial(_gather_kernel, rows=rows),
        grid=(n_t,),
        in_specs=[pl.BlockSpec((None, 1, rows), lambda t: (t, 0, 0), memory_space=pltpu.SMEM),
                  pl.BlockSpec(memory_space=pl.ANY)],
        out_specs=pl.BlockSpec((rows, w), lambda t: (t, 0)),
        out_shape=jax.ShapeDtypeStruct((n_t * rows, w), src.dtype),
        scratch_shapes=[pltpu.SemaphoreType.DMA(())],
        compiler_params=_params(("arbitrary",), 4 * _nbytes((rows, w), src.dtype)),
        name="gather_rows",
    )(idx.reshape(n_t, 1, rows), src)


def _moe_up_kernel(te_ref, x_ref, wg_ref, wu_ref, o_ref, xb_ref):
    half = x_ref.shape[1]

    @pl.when(pl.program_id(1) == 0)
    def _():
        lo, hi = _unpack_halves(x_ref[...])
        xb_ref[:, :half] = lo.astype(BF16)
        xb_ref[:, half:] = hi.astype(BF16)

    x = xb_ref[...]
    g = jnp.dot(x, wg_ref[...], preferred_element_type=F32)
    u = jnp.dot(x, wu_ref[...], preferred_element_type=F32)
    o_ref[...] = (g * jax.nn.sigmoid(g) * u).astype(o_ref.dtype)


def _moe_up(xs, wg, wu, tile_expert):
    n_rows, half = xs.shape
    _, d, f = wg.shape
    rows = _MOE_ROWS
    tn = _divisor_tile(f, 512, _LANE)
    w_spec = pl.BlockSpec((None, d, tn), lambda t, j, te: (te[t], 0, j))
    vmem = 2 * (_nbytes((rows, half), jnp.uint32) + 2 * _nbytes((d, tn), BF16) + _nbytes((rows, tn), BF16)) \
        + _nbytes((rows, d), BF16) + 4 * _nbytes((rows, tn), F32)
    return pl.pallas_call(
        _moe_up_kernel,
        grid_spec=pltpu.PrefetchScalarGridSpec(
            num_scalar_prefetch=1,
            grid=(n_rows // rows, f // tn),
            in_specs=[pl.BlockSpec((rows, half), lambda t, j, te: (t, 0)), w_spec, w_spec],
            out_specs=pl.BlockSpec((rows, tn), lambda t, j, te: (t, j)),
            scratch_shapes=[pltpu.VMEM((rows, d), BF16)]),
        out_shape=jax.ShapeDtypeStruct((n_rows, f), BF16),
        compiler_params=_params(("arbitrary", "arbitrary"), vmem + (4 << 20)),
        name="moe_up",
    )(tile_expert, xs, wg, wu)


def _moe_down_kernel(te_ref, a_ref, wlo_ref, whi_ref, o_ref):
    a = a_ref[...]
    lo = jnp.dot(a, wlo_ref[...], preferred_element_type=F32)
    hi = jnp.dot(a, whi_ref[...], preferred_element_type=F32)
    o_ref[...] = _pack_halves(lo, hi)


def _moe_down(act, wd, tile_expert):
    n_rows, f = act.shape
    d = wd.shape[2]
    half = d // 2
    rows = _MOE_ROWS
    tn = _divisor_tile(half, 512, _LANE)
    nj = half // tn
    vmem = 2 * (_nbytes((rows, f), BF16) + 2 * _nbytes((f, tn), BF16) + _nbytes((rows, tn), jnp.uint32)) \
        + 4 * _nbytes((rows, tn), F32)
    return pl.pallas_call(
        _moe_down_kernel,
        grid_spec=pltpu.PrefetchScalarGridSpec(
            num_scalar_prefetch=1,
            grid=(n_rows // rows, nj),
            in_specs=[pl.BlockSpec((rows, f), lambda t, j, te: (t, 0)),
                      pl.BlockSpec((None, f, tn), lambda t, j, te: (te[t], 0, j)),
                      pl.BlockSpec((None, f, tn), lambda t, j, te: (te[t], 0, nj + j))],
            out_specs=pl.BlockSpec((rows, tn), lambda t, j, te: (t, j))),
        out_shape=jax.ShapeDtypeStruct((n_rows, half), jnp.uint32),
        compiler_params=_params(("arbitrary", "arbitrary"), vmem + (4 << 20)),
        name="moe_down",
    )(tile_expert, act, wd, wd)


def _moe_combine_kernel(i1_ref, i2_ref, y_hbm, r_ref, h_ref, g_ref, b_ref, hf_ref, hb_ref,
                        ya_ref, yb_ref, sem, *, rows, alpha):
    def issue(r, carry):
        _row_copy(y_hbm, i1_ref[0, r], ya_ref, r, sem).start(priority=0)
        _row_copy(y_hbm, i2_ref[0, r], yb_ref, r, sem).start(priority=1)
        return carry

    lax.fori_loop(0, rows, issue, 0)
    _wait_rows(y_hbm, ya_ref, sem)
    _wait_rows(y_hbm, yb_ref, sem)

    half = ya_ref.shape[1]
    w1 = r_ref[:, 0:1]
    w2 = r_ref[:, 1:2]
    a_lo, a_hi = _unpack_halves(ya_ref[...])
    b_lo, b_hi = _unpack_halves(yb_ref[...])
    z_lo = alpha * h_ref[:, :half] + (w1 * a_lo + w2 * b_lo)
    z_hi = alpha * h_ref[:, half:] + (w1 * a_hi + w2 * b_hi)
    inv_d = 1.0 / (2 * half)
    mu = (jnp.sum(z_lo, axis=-1, keepdims=True) + jnp.sum(z_hi, axis=-1, keepdims=True)) * inv_d
    c_lo = z_lo - mu
    c_hi = z_hi - mu
    var = (jnp.sum(c_lo * c_lo, axis=-1, keepdims=True) + jnp.sum(c_hi * c_hi, axis=-1, keepdims=True)) * inv_d
    rs = lax.rsqrt(var + LN_EPS)
    y_lo = c_lo * rs * g_ref[:, :half] + b_ref[:, :half]
    y_hi = c_hi * rs * g_ref[:, half:] + b_ref[:, half:]
    hf_ref[:, :half] = y_lo
    hf_ref[:, half:] = y_hi
    hb_ref[:, :half] = y_lo.astype(BF16)
    hb_ref[:, half:] = y_hi.astype(BF16)


def _moe_combine(ypk, dest1, dest2, top2, h, g, b, alpha):
    m, d = h.shape
    rows = _divisor_tile(m, _COMBINE_ROWS, _SUBLANE)
    n_t = m // rows
    idx_spec = pl.BlockSpec((None, 1, rows), lambda t: (t, 0, 0), memory_space=pltpu.SMEM)
    row = pl.BlockSpec((rows, d), lambda t: (t, 0))
    vec = pl.BlockSpec((1, d), lambda t: (0, 0))
    vmem = 2 * (_nbytes((rows, d), F32) * 2 + _nbytes((rows, d), BF16)) + 2 * _nbytes((rows, d // 2), F32) \
        + 6 * _nbytes((rows, d), F32)
    return pl.pallas_call(
        functools.partial(_moe_combine_kernel, rows=rows, alpha=alpha),
        grid=(n_t,),
        in_specs=[idx_spec, idx_spec, pl.BlockSpec(memory_space=pl.ANY),
                  pl.BlockSpec((rows, _LANE), lambda t: (t, 0)), row, vec, vec],
        out_specs=[row, row],
        out_shape=[jax.ShapeDtypeStruct((m, d), F32), jax.ShapeDtypeStruct((m, d), BF16)],
        scratch_shapes=[pltpu.VMEM((rows, d // 2), jnp.uint32), pltpu.VMEM((rows, d // 2), jnp.uint32),
                        pltpu.SemaphoreType.DMA(())],
        compiler_params=_params(("arbitrary",), vmem),
        name="moe_combine",
    )(dest1.reshape(n_t, 1, rows), dest2.reshape(n_t, 1, rows), ypk, top2, h,
      g.reshape(1, d).astype(F32), b.reshape(1, d).astype(F32))


def _moe_routing(top2, n_exp, rows):
    m = top2.shape[0]
    choice = top2[:, 2:4].astype(jnp.int32).reshape(-1)
    onehot = (choice[:, None] == jnp.arange(n_exp, dtype=jnp.int32)[None, :]).astype(jnp.int32)
    running = jnp.cumsum(onehot, axis=0)
    rank = jnp.sum(onehot * running, axis=1) - 1
    counts = running[-1]
    padded = ((counts + rows - 1) // rows) * rows
    ends = jnp.cumsum(padded)
    starts = ends - padded
    n_tiles = -(-2 * m // rows) + n_exp
    dest = jnp.clip(jnp.sum(onehot * starts[None, :], axis=1) + rank, 0, n_tiles * rows - 1)
    src_tok = jnp.zeros((n_tiles * rows,), jnp.int32).at[dest].set(
        jnp.arange(2 * m, dtype=jnp.int32) // 2)
    tile_start = jnp.arange(n_tiles, dtype=jnp.int32) * rows
    tile_expert = jnp.minimum(jnp.sum((tile_start[:, None] >= ends[None, :]).astype(jnp.int32), axis=1),
                              n_exp - 1)
    return src_tok, tile_expert, dest[0::2], dest[1::2]


def _attn_kernel(slopes_ref, q_ref, k_ref, v_ref, lq1_ref, lk1_ref, lq2_ref, lk2_ref, g_ref,
                 o_ref, qqt_ref, vt_ref, s_ref, m_ref, l_ref, acc_ref, *, tq, tk, qk_dim, scale, lam_init):
    h = pl.program_id(1)
    i = pl.program_id(2)
    slope = slopes_ref[h]
    r = tq // tk
    lp = k_ref.shape[0]
    vdim = v_ref.shape[1]

    @pl.when(i == 0)
    def _():
        def xpose(c, carry):
            c0 = pl.multiple_of(c * tk, tk)
            vt_ref[:, pl.ds(c0, tk)] = v_ref[pl.ds(c0, tk), :].astype(F32).T.astype(BF16)
            return carry
        lax.fori_loop(0, lp // tk, xpose, 0)

    log2e = math.log2(math.e)
    qs = q_ref[...].astype(F32) * (scale * log2e)
    lane = lax.broadcasted_iota(jnp.int32, qs.shape, 1)
    ones = jnp.where(lane < 3, 1.0, 0.0).astype(F32)
    qa1 = jnp.concatenate([jnp.where(lane < qk_dim, qs, 0.0), ones], axis=1)
    qa2 = jnp.concatenate([jnp.where(lane >= qk_dim, qs, 0.0), ones], axis=1)
    qqt_ref[:, :tq] = qa1.T.astype(BF16)
    qqt_ref[:, tq:] = qa2.T.astype(BF16)
    slope2 = slope * log2e
    kj_idx = lax.broadcasted_iota(jnp.int32, (tk, vdim), 0).astype(F32)
    kl_idx = lax.broadcasted_iota(jnp.int32, (tk, vdim), 1)
    bias = slope2 * kj_idx
    b_hi = bias.astype(BF16).astype(F32)
    rem = bias - b_hi
    b_mid = rem.astype(BF16).astype(F32)
    b_lo = rem - b_mid
    kb = jnp.where(kl_idx == 0, b_hi, jnp.where(kl_idx == 1, b_mid, jnp.where(kl_idx == 2, b_lo, 0.0)))
    kb = kb.astype(BF16)

    m_ref[...] = jnp.full(m_ref.shape, _NEG, F32)
    l_ref[...] = jnp.zeros(l_ref.shape, F32)
    acc_ref[...] = jnp.zeros(acc_ref.shape, F32)

    def put_scores(slot, j):
        k0 = pl.multiple_of(j * tk, tk)
        ka = jnp.concatenate([k_ref[pl.ds(k0, tk), :], kb], axis=1)
        for half in range(2):
            s_ref[slot, half] = jnp.dot(ka, qqt_ref[:, half * tq:(half + 1) * tq],
                                        preferred_element_type=F32)

    def absorb(j, slot, masked):
        k0 = pl.multiple_of(j * tk, tk)
        vt = vt_ref[:, pl.ds(k0, tk)]
        off = j * tk - i * tq
        c = slope2 * off.astype(F32)
        for half in range(2):
            s = s_ref[slot, half]
            if masked:
                krow = lax.broadcasted_iota(jnp.int32, s.shape, 0)
                qcol = lax.broadcasted_iota(jnp.int32, s.shape, 1)
                s = jnp.where(krow - qcol <= -off, s, _NEG)
            m_old = m_ref[half]
            m_new = jnp.maximum(m_old, jnp.max(s, axis=0, keepdims=True) + c)
            p = jnp.exp2(s + (c - m_new))
            alpha = jnp.exp2(m_old - m_new)
            l_ref[half] = alpha * l_ref[half] + jnp.sum(p, axis=0, keepdims=True)
            pv = jnp.dot(vt, p.astype(BF16), preferred_element_type=F32)
            acc_ref[half] = alpha * acc_ref[half] + pv
            m_ref[half] = m_new

    n_full = i * r
    odd = n_full & 1

    @pl.when(odd == 1)
    def _():
        put_scores(0, 0)
        absorb(0, 0, False)

    put_scores(0, odd)

    def pair(t, carry):
        j = odd + 2 * t
        put_scores(1, j + 1)
        absorb(j, 0, False)
        put_scores(0, j + 2)
        absorb(j + 1, 1, False)
        return carry

    lax.fori_loop(0, lax.shift_right_logical(n_full - odd, 1), pair, 0)
    for d in range(r):
        if d + 1 < r:
            put_scores((d + 1) % 2, n_full + d + 1)
        absorb(n_full + d, d % 2, True)

    lam = (jnp.exp(jnp.sum(lq1_ref[...] * lk1_ref[...], axis=1, keepdims=True))
           - jnp.exp(jnp.sum(lq2_ref[...] * lk2_ref[...], axis=1, keepdims=True)) + lam_init)
    ot = acc_ref[0] * (1.0 / l_ref[0]) - lam * (acc_ref[1] * (1.0 / l_ref[1]))
    ms = jnp.mean(ot * ot, axis=0, keepdims=True)
    y = ot * lax.rsqrt(ms + LN_EPS) * g_ref[...] * (1.0 - lam_init)
    o_ref[...] = y.T.astype(o_ref.dtype)


def _diff_attention(proj, bsz, lp, d_att, lam_vecs, subln_g, lam_init):
    vdim = d_att // ATT_HEADS
    qk_dim = vdim // 2
    assert vdim == _LANE
    tq = _divisor_tile(lp, _TS_TARGET, _SEQ_ALIGN)
    tk = _divisor_tile(tq, _TK_ATT, _LANE)
    nq = lp // tq
    slopes = jnp.asarray(2.0 ** (-8.0 * np.arange(1, ATT_HEADS + 1) / ATT_HEADS), F32)
    lq1, lk1, lq2, lk2 = [v.reshape(1, qk_dim).astype(F32) for v in lam_vecs]
    vec = pl.BlockSpec((1, qk_dim), lambda b, h, i: (0, 0))
    in_specs = [
        pl.BlockSpec(memory_space=pltpu.SMEM),
        pl.BlockSpec((tq, vdim), lambda b, h, i: (b * nq + i, h)),
        pl.BlockSpec((lp, vdim), lambda b, h, i: (b, ATT_HEADS + h)),
        pl.BlockSpec((lp, vdim), lambda b, h, i: (b, 2 * ATT_HEADS + h)),
        vec, vec, vec, vec,
        pl.BlockSpec((vdim, 1), lambda b, h, i: (0, 0)),
    ]
    vmem = 2 * (2 * _nbytes((lp, vdim), BF16) + 2 * _nbytes((tq, vdim), BF16)) \
        + _nbytes((lp, vdim), BF16) + 4 * _nbytes((2 * tq, vdim), F32) + 6 * _nbytes((2 * tq, tk), F32)
    return pl.pallas_call(
        functools.partial(_attn_kernel, tq=tq, tk=tk, qk_dim=qk_dim, scale=qk_dim ** -0.5,
                          lam_init=lam_init),
        grid=(bsz, ATT_HEADS, nq),
        in_specs=in_specs,
        out_specs=pl.BlockSpec((tq, vdim), lambda b, h, i: (b * nq + i, h)),
        out_shape=jax.ShapeDtypeStruct((bsz * lp, d_att), BF16),
        scratch_shapes=[pltpu.VMEM((2 * vdim, 2 * tq), BF16), pltpu.VMEM((vdim, lp), BF16),
                        pltpu.VMEM((2, 2, tk, tq), F32),
                        pltpu.VMEM((2, 1, tq), F32), pltpu.VMEM((2, 1, tq), F32),
                        pltpu.VMEM((2, vdim, tq), F32)],
        compiler_params=_params(("parallel", "parallel", "arbitrary"), vmem + (8 << 20)),
        name="diff_attention",
    )(slopes, proj, proj, proj, lq1, lk1, lq2, lk2, subln_g.reshape(vdim, 1).astype(F32))


_CONV_HALO = 32
_CONV_ROWS = 32


def _conv_kernel(a_ref, gate_ref, dw_ref, dwb_ref, lng_ref, lnb_ref, pw_ref, pwb_ref, o_ref,
                 hbuf_ref, cbuf_ref, sh_ref, *, ts, k_len):
    t = pl.program_id(1)
    c_dim = a_ref.shape[1]

    @pl.when(t == 0)
    def _():
        hbuf_ref[0:_CONV_HALO, :] = jnp.zeros((_CONV_HALO, c_dim), F32)

    @pl.when(t > 0)
    def _():
        hbuf_ref[0:_CONV_HALO, :] = hbuf_ref[ts:ts + _CONV_HALO, :]

    a = a_ref[...].astype(F32)
    hbuf_ref[_CONV_HALO:_CONV_HALO + ts, :] = a * jax.nn.sigmoid(gate_ref[...].astype(F32))

    first = _CONV_HALO - (k_len - 1)
    sh_rows = sh_ref.shape[1]
    for lb in range(c_dim // _LANE):
        lanes = slice(lb * _LANE, (lb + 1) * _LANE)
        for b in range(1, _SUBLANE):
            sh_ref[b - 1] = hbuf_ref[b:b + sh_rows, lanes]
        taps = [dw_ref[k:k + 1, lanes] for k in range(k_len)]
        bias = dwb_ref[:, lanes]

        def chunk(c, carry, lanes=lanes, taps=taps, bias=bias):
            base = c * _CONV_ROWS
            acc = jnp.broadcast_to(bias, (_CONV_ROWS, _LANE))
            for k in range(k_len):
                a, b = divmod(first + k, _SUBLANE)
                start = pl.multiple_of(base + a * _SUBLANE, _SUBLANE)
                if b == 0:
                    src = hbuf_ref[pl.ds(start, _CONV_ROWS), lanes]
                else:
                    src = sh_ref[b - 1, pl.ds(start, _CONV_ROWS), :]
                acc = acc + taps[k] * src
            cbuf_ref[pl.ds(pl.multiple_of(base, _CONV_ROWS), _CONV_ROWS), lanes] = acc
            return carry

        lax.fori_loop(0, ts // _CONV_ROWS, chunk, 0)

    y = cbuf_ref[...]
    mu = jnp.mean(y, axis=-1, keepdims=True)
    yc = y - mu
    var = jnp.mean(yc * yc, axis=-1, keepdims=True)
    yn = yc * lax.rsqrt(var + LN_EPS) * lng_ref[...] + lnb_ref[...]
    act = (yn * jax.nn.sigmoid(yn)).astype(BF16)
    out = jnp.dot(act, pw_ref[...], preferred_element_type=F32) + pwb_ref[...]
    o_ref[...] = out.astype(o_ref.dtype)


def _conformer_conv(proj, bsz, lp, col0, d_conv, dw_w, dw_b, ln_g, ln_b, pw_w, pw_b):
    ts = _divisor_tile(lp, _TS_TARGET, _CONV_ROWS)
    nt = lp // ts
    cb = col0 // d_conv
    assert col0 % d_conv == 0 and dw_w.shape[0] - 1 <= _CONV_HALO
    vec = pl.BlockSpec((1, d_conv), lambda b, t: (0, 0))
    in_specs = [
        pl.BlockSpec((ts, d_conv), lambda b, t: (b * nt + t, cb)),
        pl.BlockSpec((ts, d_conv), lambda b, t: (b * nt + t, cb + 1)),
        pl.BlockSpec((dw_w.shape[0], d_conv), lambda b, t: (0, 0)),
        vec, vec, vec,
        pl.BlockSpec((d_conv, d_conv), lambda b, t: (0, 0)),
        vec,
    ]
    vmem = 4 * _nbytes((ts, d_conv), BF16) * 2 + 8 * _nbytes((ts + _CONV_HALO, d_conv), F32)
    r1 = lambda v: v.reshape(1, d_conv).astype(F32)
    return pl.pallas_call(
        functools.partial(_conv_kernel, ts=ts, k_len=dw_w.shape[0]),
        grid=(bsz, nt),
        in_specs=in_specs,
        out_specs=pl.BlockSpec((ts, d_conv), lambda b, t: (b * nt + t, 0)),
        out_shape=jax.ShapeDtypeStruct((bsz * lp, d_conv), BF16),
        scratch_shapes=[pltpu.VMEM((ts + _CONV_HALO, d_conv), F32), pltpu.VMEM((ts, d_conv), F32),
                        pltpu.VMEM((_SUBLANE - 1, ts + _CONV_HALO - _SUBLANE, _LANE), F32)],
        compiler_params=_params(("parallel", "arbitrary"), vmem),
        name="conformer_conv",
    )(proj, proj, dw_w.astype(F32), r1(dw_b), r1(ln_g), r1(ln_b), pw_w.astype(BF16), r1(pw_b))


_POOL_HALO = 16


def _pool_kernel(u_ref, w_ref, b_ref, s_ref, o_ref, buf_ref, *, ts, windows, gw):
    t = pl.program_id(1)
    c_dim = u_ref.shape[1]

    @pl.when(t == 0)
    def _():
        buf_ref[0:_POOL_HALO, :] = jnp.zeros((_POOL_HALO, c_dim), F32)

    @pl.when(t > 0)
    def _():
        buf_ref[0:_POOL_HALO, :] = buf_ref[ts:ts + _POOL_HALO, :]

    buf_ref[_POOL_HALO:_POOL_HALO + ts, :] = u_ref[...].astype(F32)
    pos = (t * ts + 1 + lax.broadcasted_iota(jnp.int32, (ts, 1), 0)).astype(F32)
    for gi, win in enumerate(windows):
        lanes = slice(gi * gw, (gi + 1) * gw)
        u = buf_ref[_POOL_HALO:_POOL_HALO + ts, lanes]
        tot = u
        for j in range(1, win):
            tot = tot + buf_ref[_POOL_HALO - j:_POOL_HALO - j + ts, lanes]
        d = tot / jnp.minimum(pos, float(win)) - u
        y = jnp.dot(d.astype(BF16), w_ref[gi], preferred_element_type=F32)
        o_ref[:, lanes] = ((y + b_ref[:, lanes]) * s_ref[:, lanes]).astype(o_ref.dtype)


def _multiscale_pool(proj, bsz, lp, col0, d_pool, w, bias, scale):
    n_g = len(POOL_WINDOWS)
    gw = d_pool // n_g
    ts = _divisor_tile(lp, _TS_TARGET, _SUBLANE)
    nt = lp // ts
    assert col0 % d_pool == 0 and max(POOL_WINDOWS) - 1 <= _POOL_HALO
    cb = col0 // d_pool
    vec = pl.BlockSpec((1, d_pool), lambda b, t: (0, 0))
    r1 = lambda v: v.reshape(1, d_pool).astype(F32)
    return pl.pallas_call(
        functools.partial(_pool_kernel, ts=ts, windows=POOL_WINDOWS, gw=gw),
        grid=(bsz, nt),
        in_specs=[pl.BlockSpec((ts, d_pool), lambda b, t: (b * nt + t, cb)),
                  pl.BlockSpec((n_g, gw, gw), lambda b, t: (0, 0, 0)), vec, vec],
        out_specs=pl.BlockSpec((ts, d_pool), lambda b, t: (b * nt + t, 0)),
        out_shape=jax.ShapeDtypeStruct((bsz * lp, d_pool), BF16),
        scratch_shapes=[pltpu.VMEM((ts + _POOL_HALO, d_pool), F32)],
        compiler_params=_params(("parallel", "arbitrary"), 10 * _nbytes((ts, d_pool), F32)),
        name="multiscale_pool",
    )(proj, w.astype(BF16), r1(bias), r1(scale))


_SSD_HALO = 8


def _bf16_terms(x):
    hi = x.astype(BF16)
    rem = x - hi.astype(F32)
    mid = rem.astype(BF16)
    lo = (rem - mid.astype(F32)).astype(BF16)
    return hi, mid, lo


def _ssd_kernel(z_ref, x_ref, bc_ref, dt_ref, cw_ref, cb_ref, dtb_ref, alog_ref, dskip_ref, ng_ref,
                expand_ref, o_ref, ext_ref, state_ref, *, tc, d_ssm, k_len):
    c = pl.program_id(1)
    n_state = SSM_STATE
    gwid = d_ssm // SSM_GROUPS
    hpg = gwid // SSM_HEADDIM
    xbc_dim = x_ref.shape[1] + bc_ref.shape[1]

    @pl.when(c == 0)
    def _():
        ext_ref[0:_SSD_HALO, :] = jnp.zeros((_SSD_HALO, xbc_dim), F32)
        state_ref[...] = jnp.zeros(state_ref.shape, F32)

    @pl.when(c > 0)
    def _():
        ext_ref[0:_SSD_HALO, :] = ext_ref[tc:tc + _SSD_HALO, :]

    ext_ref[_SSD_HALO:_SSD_HALO + tc, :d_ssm] = x_ref[...].astype(F32)
    ext_ref[_SSD_HALO:_SSD_HALO + tc, d_ssm:] = bc_ref[...].astype(F32)

    first = _SSD_HALO - (k_len - 1)
    pre = jnp.broadcast_to(cb_ref[...], (tc, xbc_dim))
    for k in range(k_len):
        pre = pre + cw_ref[k:k + 1, :] * ext_ref[first + k:first + k + tc, :]
    xbc = pre * jax.nn.sigmoid(pre)
    xs = xbc[:, :d_ssm]
    bm = xbc[:, d_ssm:d_ssm + SSM_GROUPS * n_state].astype(BF16)
    cm = xbc[:, d_ssm + SSM_GROUPS * n_state:].astype(BF16)

    dt = jax.nn.softplus(dt_ref[...] + dtb_ref[...])
    a = dt * (-jnp.exp(alog_ref[...]))
    ti = lax.broadcasted_iota(jnp.int32, (tc, tc), 0)
    si = lax.broadcasted_iota(jnp.int32, (tc, tc), 1)
    causal = ti >= si
    tri = jnp.where(causal, 1.0, 0.0).astype(BF16)
    a_cs = sum(jnp.dot(tri, part, preferred_element_type=F32) for part in _bf16_terms(a))
    a_cs_t = a_cs.T
    expand = expand_ref[...]
    a_cs_x = sum(jnp.dot(part, expand, preferred_element_type=F32) for part in _bf16_terms(a_cs))
    dt_x = sum(jnp.dot(part, expand, preferred_element_type=F32) for part in _bf16_terms(dt))
    a_last = a_cs_x[tc - 1:tc, :]
    x_dt = xs * dt_x
    x_end = (x_dt * jnp.exp(a_last - a_cs_x)).astype(BF16)
    x_dt_b = x_dt.astype(BF16)
    decay_in = jnp.exp(a_cs_x)
    decay_chunk = jnp.exp(a_last)

    lane_g = lax.broadcasted_iota(jnp.int32, (tc, gwid), 1)
    ys = []
    for g in range(SSM_GROUPS):
        cg = cm[:, g * n_state:(g + 1) * n_state]
        bg = bm[:, g * n_state:(g + 1) * n_state]
        gl = slice(g * gwid, (g + 1) * gwid)
        cb = lax.dot_general(cg, bg, (((1,), (1,)), ((), ())), preferred_element_type=F32)
        xg = x_dt_b[:, gl]
        y_g = None
        for r in range(hpg):
            hd = g * hpg + r
            seg = a_cs[:, hd:hd + 1] - a_cs_t[hd:hd + 1, :]
            lmat = jnp.exp(jnp.where(causal, seg, _NEG))
            mh = (cb * lmat).astype(BF16)
            in_head = (lane_g >= r * SSM_HEADDIM) & (lane_g < (r + 1) * SSM_HEADDIM)
            xh = jnp.where(in_head, xg, jnp.zeros_like(xg))
            yh = jnp.dot(mh, xh, preferred_element_type=F32)
            y_g = yh if y_g is None else y_g + yh
        st = state_ref[g]
        y_g = y_g + jnp.dot(cg, st.astype(BF16), preferred_element_type=F32) * decay_in[:, gl]
        upd = lax.dot_general(bg, x_end[:, gl], (((0,), (0,)), ((), ())), preferred_element_type=F32)
        state_ref[g] = decay_chunk[:, gl] * st + upd
        ys.append(y_g)
    y = jnp.concatenate(ys, axis=1) + dskip_ref[...] * xs
    zf = z_ref[...].astype(F32)
    y = y * (zf * jax.nn.sigmoid(zf))
    outs = []
    for g in range(SSM_GROUPS):
        yg = y[:, g * gwid:(g + 1) * gwid]
        ms = jnp.mean(yg * yg, axis=-1, keepdims=True)
        outs.append(yg * lax.rsqrt(ms + LN_EPS))
    o_ref[...] = (jnp.concatenate(outs, axis=1) * ng_ref[...]).astype(o_ref.dtype)


def _mamba2_ssd(proj, dt_raw, bsz, lp, z_col0, d_ssm, conv_w, conv_b, dt_bias, a_log, d_skip, norm_g):
    xbc_dim = d_ssm + 2 * SSM_GROUPS * SSM_STATE
    n_heads = d_ssm // SSM_HEADDIM
    tc = _SSD_CHUNK
    nc = lp // tc
    bc_dim = xbc_dim - d_ssm
    assert lp % tc == 0 and z_col0 % d_ssm == 0 and (z_col0 + 2 * d_ssm) % bc_dim == 0
    zb = z_col0 // d_ssm
    bcb = (z_col0 + 2 * d_ssm) // bc_dim
    pad_h = lambda v: jnp.pad(v.astype(F32).reshape(1, n_heads), ((0, 0), (0, _LANE - n_heads)))
    expand = (np.arange(d_ssm)[None, :] // SSM_HEADDIM == np.arange(_LANE)[:, None]).astype(np.float32)
    d_x = jnp.repeat(d_skip.astype(F32), SSM_HEADDIM).reshape(1, d_ssm)
    full = lambda shape: pl.BlockSpec(shape, lambda b, c: (0,) * len(shape))
    in_specs = [
        pl.BlockSpec((tc, d_ssm), lambda b, c: (b * nc + c, zb)),
        pl.BlockSpec((tc, d_ssm), lambda b, c: (b * nc + c, zb + 1)),
        pl.BlockSpec((tc, bc_dim), lambda b, c: (b * nc + c, bcb)),
        pl.BlockSpec((tc, _LANE), lambda b, c: (b * nc + c, 0)),
        full((conv_w.shape[0], xbc_dim)), full((1, xbc_dim)),
        full((1, _LANE)), full((1, _LANE)), full((1, d_ssm)), full((1, d_ssm)),
        full((_LANE, d_ssm)),
    ]
    return pl.pallas_call(
        functools.partial(_ssd_kernel, tc=tc, d_ssm=d_ssm, k_len=conv_w.shape[0]),
        grid=(bsz, nc),
        in_specs=in_specs,
        out_specs=pl.BlockSpec((tc, d_ssm), lambda b, c: (b * nc + c, 0)),
        out_shape=jax.ShapeDtypeStruct((bsz * lp, d_ssm), BF16),
        scratch_shapes=[pltpu.VMEM((tc + _SSD_HALO, xbc_dim), F32),
                        pltpu.VMEM((SSM_GROUPS, SSM_STATE, d_ssm // SSM_GROUPS), F32)],
        compiler_params=_params(("parallel", "arbitrary"), 32 << 20),
        name="mamba2_ssd",
    )(proj, proj, proj, dt_raw, conv_w.astype(F32), conv_b.reshape(1, xbc_dim).astype(F32),
      pad_h(dt_bias), pad_h(a_log), d_x, norm_g.reshape(1, d_ssm).astype(F32),
      jnp.asarray(expand, dtype=BF16))


def kernel(x, meta, emb_ln_g, emb_ln_b, w_in, lam_q1, lam_k1, lam_q2, lam_k2, att_subln_g, cf_dw_w, cf_dw_b, cf_ln_g, cf_ln_b, cf_pw_w, cf_pw_b, pool_w, pool_b, pool_scale, ssm_conv_w, ssm_conv_b, ssm_dt_bias, ssm_a_log, ssm_d, ssm_norm_g, w_out, ln1_g, ln1_b, ln2_g, ln2_b, ffn_w_gate, ffn_w_up, ffn_w_down, moe_router, moe_w_gate, moe_w_up, moe_w_down):
    bsz, seq, d_model = x.shape
    depth = w_in.shape[0]
    n_meta = meta.shape[0]
    d_att = d_conv = d_pool = d_ssm = d_model // 4
    xbc_dim = d_ssm + 2 * SSM_GROUPS * SSM_STATE
    d_main = 3 * d_att + 2 * d_conv + d_pool + d_ssm + xbc_dim
    alpha = (2 * depth) ** 0.25

    seq_len = n_meta + seq
    lp = -(-seq_len // _SEQ_ALIGN) * _SEQ_ALIGN
    h, h16 = _embed_layer_norm(x, meta, emb_ln_g, emb_ln_b, lp)

    w_in_t = jnp.swapaxes(w_in, 1, 2)
    for l in range(depth):
        proj = _matmul([h16], _weight_t_bf16(w_in_t, l, d_main), BF16)
        dt_raw = _matmul([h16], _weight_t_tail_bf16(w_in_t, l, d_main), F32)
        lam_init = 0.8 - 0.6 * math.exp(-0.3 * l)

        y_att = _diff_attention(proj, bsz, lp, d_att, (lam_q1[l], lam_k1[l], lam_q2[l], lam_k2[l]),
                                att_subln_g[l], lam_init)
        y_conv = _conformer_conv(proj, bsz, lp, 3 * d_att, d_conv, cf_dw_w[l], cf_dw_b[l],
                                 cf_ln_g[l], cf_ln_b[l], cf_pw_w[l], cf_pw_b[l])
        y_pool = _multiscale_pool(proj, bsz, lp, 3 * d_att + 2 * d_conv, d_pool,
                                  pool_w[l], pool_b[l], pool_scale[l])
        y_ssm = _mamba2_ssd(proj, dt_raw, bsz, lp, 3 * d_att + 2 * d_conv + d_pool, d_ssm,
                            ssm_conv_w[l], ssm_conv_b[l], ssm_dt_bias[l], ssm_a_log[l], ssm_d[l],
                            ssm_norm_g[l])
        z = _matmul([y_att, y_conv, y_pool, y_ssm], _weight_bf16(w_out, l, 1)[0], F32,
                    residual=h, alpha=alpha)

        e = l // 2
        if l % 2 == 0:
            h, h16 = _layer_norm(z, ln1_g[l], ln1_b[l])
            act = _swiglu_up(h16, _weight_bf16(ffn_w_gate, e, 1)[0], _weight_bf16(ffn_w_up, e, 1)[0])
            z = _matmul([act], _weight_bf16(ffn_w_down, e, 1)[0], F32, residual=h, alpha=alpha)
            h, h16 = _layer_norm(z, ln2_g[l], ln2_b[l])
        else:
            n_exp = moe_router.shape[2]
            expert_bf16 = lambda w: _weight_bf16(w.reshape((-1,) + w.shape[2:]), e * n_exp, n_exp)
            h, h16, h_packed = _layer_norm(z, ln1_g[l], ln1_b[l], packed=True)
            top2 = _router_top2(h, moe_router[e])
            src_tok, tile_expert, dest1, dest2 = _moe_routing(top2, n_exp, _MOE_ROWS)
            xs = _gather_rows(h_packed, src_tok, _MOE_ROWS)
            act = _moe_up(xs, expert_bf16(moe_w_gate), expert_bf16(moe_w_up), tile_expert)
            y_sorted = _moe_down(act, expert_bf16(moe_w_down), tile_expert)
            h, h16 = _moe_combine(y_sorted, dest1, dest2, top2, h, ln2_g[l], ln2_b[l], alpha)

    return h.reshape(bsz, lp, d_model)[:, n_meta:seq_len]
```
